```python
import math
import jax, jax.numpy as jnp
from jax import lax
import numpy as np

D_MODEL = 2048
BATCH = 1
SEQ = 8192
DEPTH = 4

GRID_W = 64
CTX_LEN = 256
N_MIXERS = 4
ALPHA = (2.0 * DEPTH) ** 0.25
BETA = (8.0 * DEPTH) ** -0.25
Q_BLOCK = 128
ROPE_THETA = 10000.0
LN_EPS = 1e-6
RMS_EPS = 1e-6
N_MOD = 6

HEAD_DIM = 128
GQA_HEADS = D_MODEL // HEAD_DIM
GQA_KV_HEADS = GQA_HEADS // 4
GQA_GROUP = GQA_HEADS // GQA_KV_HEADS
GQA_QKV = (GQA_HEADS + 2 * GQA_KV_HEADS) * HEAD_DIM
DIFF_HEADS = D_MODEL // (2 * HEAD_DIM)

POOL_WINDOWS = (2, 4, 8, 16)
POOL_GROUP = D_MODEL // len(POOL_WINDOWS)

N_GROUPS = 4
EXPERTS_PER_GROUP = 4
N_EXPERTS = N_GROUPS * EXPERTS_PER_GROUP
TOP_K = 2
EXPERT_FF = 3 * D_MODEL // 8

kernel_name = "hybrid_interleaved_diffusion_trunk"


def _n_uses(mixer):
    return len(range(mixer, DEPTH, N_MIXERS))


def layer_norm(x, g, b):
    xf = x.astype(jnp.float32)
    mu = jnp.mean(xf, axis=-1, keepdims=True)
    var = jnp.mean(jnp.square(xf - mu), axis=-1, keepdims=True)
    return ((xf - mu) * lax.rsqrt(var + LN_EPS)).astype(x.dtype) * g + b


def rms_norm(x, g):
    xf = x.astype(jnp.float32)
    return (xf * lax.rsqrt(jnp.mean(xf * xf, axis=-1, keepdims=True) + RMS_EPS)).astype(x.dtype) * g


def modulate(t, shift, scale):
    return t * (1.0 + scale) + shift


def axial_rope_tables(rows, dtype):
    row = jnp.repeat(jnp.arange(rows, dtype=jnp.float32), GRID_W)
    col = jnp.tile(jnp.arange(GRID_W, dtype=jnp.float32), rows)
    axis_dim = HEAD_DIM // 2
    inv = ROPE_THETA ** (-jnp.arange(0, axis_dim, 2, dtype=jnp.float32) / axis_dim)
    ang = jnp.concatenate([row[:, None] * inv, col[:, None] * inv], axis=-1)
    return jnp.cos(ang).astype(dtype), jnp.sin(ang).astype(dtype)


def apply_axial_rope(x, cos, sin):
    qd = x.shape[-1] // 4
    xr1, xr2, xc1, xc2 = jnp.split(x, 4, axis=-1)
    cr, cc = cos[:, :qd], cos[:, qd:]
    sr, sc = sin[:, :qd], sin[:, qd:]
    return jnp.concatenate([xr1 * cr - xr2 * sr, xr2 * cr + xr1 * sr,
                            xc1 * cc - xc2 * sc, xc2 * cc + xc1 * sc], axis=-1)


def sweep_query_blocks(fn, q):
    n = q.shape[-2]
    qb = jnp.moveaxis(q.reshape(q.shape[:-2] + (n // Q_BLOCK, Q_BLOCK, q.shape[-1])), -3, 0)
    ob = jnp.moveaxis(lax.map(fn, qb), 0, -3)
    return ob.reshape(ob.shape[:-3] + (n, ob.shape[-1]))


def depthwise_conv3(u, w):
    return lax.conv_general_dilated(u, w[:, None, :], window_strides=(1,), padding=((1, 1),),
                                    dimension_numbers=('NWC', 'WIO', 'NWC'),
                                    feature_group_count=u.shape[-1])


def conv_mixer(h, hc, w_in, w_conv, w_out, need_ctx):
    def run(t):
        gate_b, gate_c, val = jnp.split(t @ w_in, 3, axis=-1)
        return (gate_b * depthwise_conv3(gate_c * val, w_conv)) @ w_out
    return run(h), (run(hc) if need_ctx else None)


def window_mean_minus_self(u, window):
    n = u.shape[1]
    cs = jnp.pad(jnp.cumsum(u.astype(jnp.float32), axis=1), ((0, 0), (1, 0), (0, 0)))
    t = jnp.arange(n)
    lo = jnp.clip(t - window // 2, 0, n)
    hi = jnp.clip(t + window - window // 2, 0, n)
    mean = (cs[:, hi] - cs[:, lo]) / (hi - lo).astype(jnp.float32)[None, :, None]
    return mean.astype(u.dtype) - u


def pool_mixer(h, hc, w_grp, scale, need_ctx):
    def run(t):
        parts = jnp.split(t, len(POOL_WINDOWS), axis=-1)
        pooled = jnp.stack([window_mean_minus_self(p, w) for p, w in zip(parts, POOL_WINDOWS)], axis=2)
        mixed = jnp.einsum('bngc,gce->bnge', pooled, w_grp)
        return mixed.reshape(t.shape) * scale
    return run(h), (run(hc) if need_ctx else None)


def gqa_mixer(h, hc, w_qkv, qk_norm, w_out, cos, sin, need_ctx):
    b = h.shape[0]
    scale = HEAD_DIM ** -0.5

    def project(t):
        n = t.shape[1]
        q, k, v = jnp.split(t @ w_qkv, [GQA_HEADS * HEAD_DIM, (GQA_HEADS + GQA_KV_HEADS) * HEAD_DIM], axis=-1)
        q = q.reshape(b, n, GQA_KV_HEADS, GQA_GROUP, HEAD_DIM).transpose(0, 2, 3, 1, 4)
        k = k.reshape(b, n, GQA_KV_HEADS, HEAD_DIM).transpose(0, 2, 1, 3)
        v = v.reshape(b, n, GQA_KV_HEADS, HEAD_DIM).transpose(0, 2, 1, 3)
        return rms_norm(q, qk_norm[0]), rms_norm(k, qk_norm[1]), v

    def attend(qb, k, v):
        s = jnp.einsum('bkgqd,bkld->bkgql', qb, k).astype(jnp.float32) * scale
        p = jax.nn.softmax(s, axis=-1).astype(v.dtype)
        return jnp.einsum('bkgql,bkld->bkgqd', p, v)

    def finish(o):
        return o.transpose(0, 3, 1, 2, 4).reshape(b, o.shape[3], -1) @ w_out

    q, k, v = project(h)
    qc, kc, vc = project(hc)
    q = apply_axial_rope(q, cos, sin)
    k = apply_axial_rope(k, cos, sin)
    k_all = jnp.concatenate([kc, k], axis=2)
    v_all = jnp.concatenate([vc, v], axis=2)
    out = finish(sweep_query_blocks(lambda qb: attend(qb, k_all, v_all), q))
    out_c = finish(attend(qc, kc, vc)) if need_ctx else None
    return out, out_c


def diff_mixer(h, hc, w_qkv, lam_vecs, subln, w_out, lam_init, cos, sin, need_ctx):
    b = h.shape[0]
    scale = HEAD_DIM ** -0.5
    lv = lam_vecs.astype(jnp.float32)
    lam = jnp.exp(jnp.sum(lv[0] * lv[1])) - jnp.exp(jnp.sum(lv[2] * lv[3])) + lam_init

    def project(t):
        n = t.shape[1]
        q, k, v = jnp.split(t @ w_qkv, 3, axis=-1)
        q = q.reshape(b, n, DIFF_HEADS, 2, HEAD_DIM).transpose(0, 2, 3, 1, 4)
        k = k.reshape(b, n, DIFF_HEADS, 2, HEAD_DIM).transpose(0, 2, 3, 1, 4)
        v = v.reshape(b, n, DIFF_HEADS, 2 * HEAD_DIM).transpose(0, 2, 1, 3)
        return q, k, v

    def attend(qb, k, v):
        s = jnp.einsum('bhmqd,bhmld->bhmql', qb, k).astype(jnp.float32) * scale
        p = jax.nn.softmax(s, axis=-1)
        a = p[:, :, 0] - lam * p[:, :, 1]
        return jnp.einsum('bhql,bhle->bhqe', a.astype(v.dtype), v)

    def finish(o):
        o = rms_norm(o, subln) * (1.0 - lam_init)
        return o.transpose(0, 2, 1, 3).reshape(b, o.shape[2], -1) @ w_out

    q, k, v = project(h)
    kc_q = project(hc)
    qc, kc, vc = kc_q
    q = apply_axial_rope(q, cos, sin)
    k = apply_axial_rope(k, cos, sin)
    k_all = jnp.concatenate([kc, k], axis=3)
    v_all = jnp.concatenate([vc, v], axis=2)
    out = finish(sweep_query_blocks(lambda qb: attend(qb, k_all, v_all), q))
    out_c = finish(attend(qc, kc, vc)) if need_ctx else None
    return out, out_c


def hier_moe(h, w_grp, b_grp, w_exp, b_exp, w_gate, w_up, w_down):
    b, n, _ = h.shape
    lg = (h @ w_grp).astype(jnp.float32) + b_grp
    g_star = jnp.argmax(lg, axis=-1)
    p_group = jnp.take_along_axis(jax.nn.softmax(lg, axis=-1), g_star[..., None], axis=-1)
    le = ((h @ w_exp).astype(jnp.float32) + b_exp).reshape(b, n, N_GROUPS, EXPERTS_PER_GROUP)
    le = jnp.take_along_axis(le, g_star[..., None, None], axis=2)[:, :, 0]
    top_v, top_i = lax.top_k(jax.nn.softmax(le, axis=-1), TOP_K)
    top_v = top_v / jnp.sum(top_v, axis=-1, keepdims=True)
    expert_idx = g_star[..., None] * EXPERTS_PER_GROUP + top_i
    combine = jnp.sum(jax.nn.one_hot(expert_idx, N_EXPERTS, dtype=jnp.float32)
                      * (p_group * top_v)[..., None], axis=-2)
    act = jax.nn.silu(jnp.einsum('bnd,edf->bnef', h, w_gate)) * jnp.einsum('bnd,edf->bnef', h, w_up)
    return jnp.einsum('bnef,efd->bnd', act * combine.astype(h.dtype)[..., None], w_down)


def setup_inputs(seed: int = 0) -> dict:
    key = jax.random.key(seed)
    ks = iter(jax.random.split(key, 32))
    d, L = D_MODEL, DEPTH
    nA, nB, nC, nD = _n_uses(0), _n_uses(1), _n_uses(2), _n_uses(3)

    def nrm(shape, std):
        return jax.random.normal(next(ks), shape, jnp.float32) * std

    return {
        'x': nrm((BATCH, SEQ, d), 1.0),
        'c': nrm((BATCH, d), 1.0),
        'ctx': nrm((BATCH, CTX_LEN, d), 1.0),
        'c_ctx': nrm((d,), 1.0),
        'mod_w': nrm((L, d, N_MOD * d), 0.5 * d ** -0.5),
        'mod_b': nrm((L, N_MOD * d), 0.02),
        'ln_g': 1.0 + nrm((L, 2, d), 0.05),
        'ln_b': nrm((L, 2, d), 0.02),
        'rt_grp_w': nrm((L, d, N_GROUPS), d ** -0.5),
        'rt_grp_b': nrm((L, N_GROUPS), 0.01),
        'rt_exp_w': nrm((L, d, N_EXPERTS), d ** -0.5),
        'rt_exp_b': nrm((L, N_EXPERTS), 0.01),
        'ex_w_gate': nrm((L, N_EXPERTS, d, EXPERT_FF), d ** -0.5),
        'ex_w_up': nrm((L, N_EXPERTS, d, EXPERT_FF), d ** -0.5),
        'ex_w_down': nrm((L, N_EXPERTS, EXPERT_FF, d), BETA * EXPERT_FF ** -0.5),
        'conv_in_w': nrm((nA, d, 3 * d), d ** -0.5),
        'conv_w': nrm((nA, 3, d), 3 ** -0.5),
        'conv_out_w': nrm((nA, d, d), BETA * d ** -0.5),
        'pool_w': nrm((nB, len(POOL_WINDOWS), POOL_GROUP, POOL_GROUP), BETA * POOL_GROUP ** -0.5),
        'pool_scale': 1.0 + nrm((nB, d), 0.1),
        'gqa_qkv_w': nrm((nC, d, GQA_QKV), d ** -0.5),
        'gqa_qk_norm': 1.0 + nrm((nC, 2, HEAD_DIM), 0.05),
        'gqa_out_w': nrm((nC, GQA_HEADS * HEAD_DIM, d), BETA * d ** -0.5),
        'diff_qkv_w': nrm((nD, d, 3 * d), d ** -0.5),
        'diff_lambda': nrm((nD, 4, HEAD_DIM), 0.1),
        'diff_subln': 1.0 + nrm((nD, 2 * HEAD_DIM), 0.05),
        'diff_out_w': nrm((nD, d, d), BETA * d ** -0.5),
    }


def reference(x, c, ctx, c_ctx, mod_w, mod_b, ln_g, ln_b, rt_grp_w, rt_grp_b, rt_exp_w, rt_exp_b,
              ex_w_gate, ex_w_up, ex_w_down, conv_in_w, conv_w, conv_out_w, pool_w, pool_scale,
              gqa_qkv_w, gqa_qk_norm, gqa_out_w, diff_qkv_w, diff_lambda, diff_subln, diff_out_w):
    rows = x.shape[1] // GRID_W
    cos, sin = axial_rope_tables(rows, x.dtype)
    s_lat = jax.nn.silu(c)[:, None, :]
    s_ctx = jax.nn.silu(c_ctx)[None, None, :]
    xc = ctx
    for i in range(DEPTH):
        mixer, j = i % N_MIXERS, i // N_MIXERS
        need_ctx = i < DEPTH - 1
        mod = jnp.split(s_lat @ mod_w[i] + mod_b[i], N_MOD, axis=-1)
        modc = jnp.split(s_ctx @ mod_w[i] + mod_b[i], N_MOD, axis=-1)
        h = modulate(x, mod[0], mod[1])
        hc = modulate(xc, modc[0], modc[1])
        if mixer == 0:
            o, oc = conv_mixer(h, hc, conv_in_w[j], conv_w[j], conv_out_w[j], need_ctx)
        elif mixer == 1:
            o, oc = pool_mixer(h, hc, pool_w[j], pool_scale[j], need_ctx)
        elif mixer == 2:
            o, oc = gqa_mixer(h, hc, gqa_qkv_w[j], gqa_qk_norm[j], gqa_out_w[j], cos, sin, need_ctx)
        else:
            lam_init = 0.8 - 0.6 * math.exp(-0.3 * i)
            o, oc = diff_mixer(h, hc, diff_qkv_w[j], diff_lambda[j], diff_subln[j], diff_out_w[j],
                               lam_init, cos, sin, need_ctx)
        moe_w = (rt_grp_w[i], rt_grp_b[i], rt_exp_w[i], rt_exp_b[i], ex_w_gate[i], ex_w_up[i], ex_w_down[i])
        x = layer_norm(ALPHA * x + mod[2] * o, ln_g[i, 0], ln_b[i, 0])
        x = layer_norm(ALPHA * x + mod[5] * hier_moe(modulate(x, mod[3], mod[4]), *moe_w),
                       ln_g[i, 1], ln_b[i, 1])
        if need_ctx:
            xc = layer_norm(ALPHA * xc + modc[2] * oc, ln_g[i, 0], ln_b[i, 0])
            xc = layer_norm(ALPHA * xc + modc[5] * hier_moe(modulate(xc, modc[3], modc[4]), *moe_w),
                            ln_g[i, 1], ln_b[i, 1])
    return x
```

```python
import functools
import math

import jax
import jax.numpy as jnp
from jax import lax
from jax.experimental import pallas as pl
from jax.experimental.pallas import tpu as pltpu

D_MODEL = 2048
SEQ = 8192
DEPTH = 4
GRID_W = 64
CTX_LEN = 256
ALPHA = (2.0 * DEPTH) ** 0.25
ROPE_THETA = 10000.0
LN_EPS = 1e-6
RMS_EPS = 1e-6
N_MOD = 6
HEAD_DIM = 128
GQA_HEADS = D_MODEL // HEAD_DIM
GQA_KV_HEADS = GQA_HEADS // 4
GQA_GROUP = GQA_HEADS // GQA_KV_HEADS
DIFF_HEADS = D_MODEL // (2 * HEAD_DIM)
POOL_WINDOWS = (2, 4, 8, 16)
POOL_GROUP = D_MODEL // len(POOL_WINDOWS)
N_GROUPS = 4
EXPERTS_PER_GROUP = 4
N_EXPERTS = N_GROUPS * EXPERTS_PER_GROUP
EXPERT_FF = 3 * D_MODEL // 8

LANES = 128
SUBLANES = 8
TM = 256
T_ALL = CTX_LEN + SEQ
NT_ALL = T_ALL // TM
HALO = SUBLANES
KV_CHUNK = 768
VMEM_LIMIT = 56 * 1024 * 1024

assert CTX_LEN == TM and SEQ % TM == 0 and T_ALL % KV_CHUNK == 0 and DEPTH == 4

F32 = jnp.float32
BF16 = jnp.bfloat16


def _params(*sem):
    return pltpu.CompilerParams(dimension_semantics=sem, vmem_limit_bytes=VMEM_LIMIT)


def _dot(a, b):
    return jnp.dot(a, b, preferred_element_type=F32)


def _layer_norm(y, g, b):
    mu = jnp.mean(y, axis=-1, keepdims=True)
    d = y - mu
    var = jnp.mean(d * d, axis=-1, keepdims=True)
    return d * lax.rsqrt(var + LN_EPS) * g + b


def _mod_spec(kind0):
    return pl.BlockSpec((1, 1, D_MODEL), lambda i, *_: (jnp.minimum(i + kind0, 1), 0, 0))


def _row_spec(width, tile0=0):
    return pl.BlockSpec((TM, width), lambda i, *_: (i + tile0, 0))


def _full_spec(shape):
    nd = len(shape)
    return pl.BlockSpec(shape, lambda i, *_: (0,) * nd)


MOD_TN = 512


def _mod_kernel(c_ref, w_ref, b_ref, o_ref):
    w = w_ref[0]
    rows = []
    for r in range(2):
        c = c_ref[:, r:r + 1]
        s = c * jax.nn.sigmoid(c)
        rows.append(jnp.sum(w * s, axis=0, keepdims=True))
    rows.append(jnp.zeros((SUBLANES - 2, MOD_TN), F32))
    o_ref[0] = jnp.concatenate(rows, axis=0) + b_ref[0]


def _modulation(c_cols, mod_w, mod_b):
    n_out = N_MOD * D_MODEL
    return pl.pallas_call(
        _mod_kernel,
        grid=(DEPTH, n_out // MOD_TN),
        in_specs=[
            pl.BlockSpec((D_MODEL, LANES), lambda l, j: (0, 0)),
            pl.BlockSpec((1, D_MODEL, MOD_TN), lambda l, j: (l, 0, j)),
            pl.BlockSpec((1, 1, MOD_TN), lambda l, j: (l, 0, j)),
        ],
        out_specs=pl.BlockSpec((1, SUBLANES, MOD_TN), lambda l, j: (l, 0, j)),
        out_shape=jax.ShapeDtypeStruct((DEPTH, SUBLANES, n_out), F32),
        compiler_params=_params("arbitrary", "arbitrary"),
        name="modulation",
    )(c_cols, mod_w, mod_b.reshape(DEPTH, 1, n_out))


def _modulate_kernel(x_ref, sh_ref, sc_ref, o_ref):
    o_ref[...] = (x_ref[...] * (1.0 + sc_ref[0]) + sh_ref[0]).astype(o_ref.dtype)


def _modulate(x, shift, scale):
    return pl.pallas_call(
        _modulate_kernel,
        grid=(NT_ALL,),
        in_specs=[_row_spec(D_MODEL), _mod_spec(0), _mod_spec(0)],
        out_specs=_row_spec(D_MODEL),
        out_shape=jax.ShapeDtypeStruct((T_ALL, D_MODEL), BF16),
        compiler_params=_params("arbitrary"),
        name="modulate",
    )(x, shift, scale)


CONV_TN = 512


def _conv_in_kernel(h_ref, wb_ref, wc_ref, wv_ref, gb_ref, p_ref):
    h = h_ref[...]
    gb_ref[...] = _dot(h, wb_ref[...])
    p_ref[...] = _dot(h, wc_ref[...]) * _dot(h, wv_ref[...])


def _conv_in(h, w_in):
    nb = D_MODEL // CONV_TN
    wspec = lambda off: pl.BlockSpec((D_MODEL, CONV_TN), lambda j, i: (0, j + off * nb))
    ospec = pl.BlockSpec((TM, CONV_TN), lambda j, i: (i, j))
    return pl.pallas_call(
        _conv_in_kernel,
        grid=(nb, NT_ALL),
        in_specs=[pl.BlockSpec((TM, D_MODEL), lambda j, i: (i, 0)), wspec(0), wspec(1), wspec(2)],
        out_specs=[ospec, ospec],
        out_shape=[jax.ShapeDtypeStruct((T_ALL, D_MODEL), F32)] * 2,
        compiler_params=_params("arbitrary", "arbitrary"),
        name="conv_in",
    )(h, w_in, w_in, w_in)


def _halo_flags(i, nt):
    has_prev = (i >= 2).astype(F32)
    has_next = jnp.logical_and(i >= 1, i < nt - 1).astype(F32)
    return has_prev, has_next


def _conv_mix_kernel(p_ref, pp_ref, pn_ref, gb_ref, cw_ref, r_ref, ext):
    has_prev, has_next = _halo_flags(pl.program_id(0), NT_ALL)
    ext[0:HALO] = pp_ref[...] * has_prev
    ext[HALO:HALO + TM] = p_ref[...]
    ext[HALO + TM:] = pn_ref[...] * has_next
    q = (ext[HALO - 1:HALO - 1 + TM] * cw_ref[0:1]
         + ext[HALO:HALO + TM] * cw_ref[1:2]
         + ext[HALO + 1:HALO + 1 + TM] * cw_ref[2:3])
    r_ref[...] = (gb_ref[...] * q).astype(r_ref.dtype)


def _halo_specs(width):
    per = TM // HALO
    last = T_ALL // HALO - 1
    prev = pl.BlockSpec((HALO, width), lambda i: (jnp.maximum(i * per - 1, 0), 0))
    nxt = pl.BlockSpec((HALO, width), lambda i: (jnp.minimum((i + 1) * per, last), 0))
    return prev, nxt


def _conv_mix(p, gb, conv_w):
    prev, nxt = _halo_specs(D_MODEL)
    return pl.pallas_call(
        _conv_mix_kernel,
        grid=(NT_ALL,),
        in_specs=[_row_spec(D_MODEL), prev, nxt, _row_spec(D_MODEL), _full_spec((3, D_MODEL))],
        out_specs=_row_spec(D_MODEL),
        out_shape=jax.ShapeDtypeStruct((T_ALL, D_MODEL), BF16),
        scratch_shapes=[pltpu.VMEM((TM + 2 * HALO, D_MODEL), F32)],
        compiler_params=_params("arbitrary"),
        name="conv_mix",
    )(p, p, p, gb, conv_w)


def _mixer_out_kernel(r_ref, w_ref, x_ref, gate_ref, g_ref, b_ref, o_ref):
    o = _dot(r_ref[...], w_ref[...])
    y = ALPHA * x_ref[...] + gate_ref[0] * o
    o_ref[...] = _layer_norm(y, g_ref[...], b_ref[...])


def _mixer_out(r, w_out, x, gate, ln_g, ln_b, *, kind0):
    nt = r.shape[0] // TM
    return pl.pallas_call(
        _mixer_out_kernel,
        grid=(nt,),
        in_specs=[_row_spec(D_MODEL), _full_spec((D_MODEL, D_MODEL)), _row_spec(D_MODEL, NT_ALL - nt),
                  _mod_spec(kind0), _full_spec((1, D_MODEL)), _full_spec((1, D_MODEL))],
        out_specs=_row_spec(D_MODEL),
        out_shape=jax.ShapeDtypeStruct((nt * TM, D_MODEL), F32),
        compiler_params=_params("arbitrary"),
        name="mixer_out",
    )(r, w_out, x, gate, ln_g, ln_b)


def _pool_kernel(x_ref, xp_ref, xn_ref, sh_ref, sc_ref, gate_ref, w_ref, ps_ref, g_ref, b_ref, o_ref, ext):
    i = pl.program_id(0)
    has_prev, has_next = _halo_flags(i, NT_ALL)
    one_sc = 1.0 + sc_ref[0]
    sh = sh_ref[0]
    x = x_ref[...]
    h = x * one_sc + sh
    ext[0:HALO] = (xp_ref[...] * one_sc + sh) * has_prev
    ext[HALO:HALO + TM] = h
    ext[HALO + TM:] = (xn_ref[...] * one_sc + sh) * has_next
    t_loc = lax.broadcasted_iota(jnp.int32, (TM, 1), 0) + jnp.maximum(i - 1, 0) * TM
    n_seq = jnp.where(i == 0, CTX_LEN, SEQ)
    mixed = []
    for g, win in enumerate(POOL_WINDOWS):
        c0 = g * POOL_GROUP
        half = win // 2
        s = ext[HALO - half:HALO - half + TM, c0:c0 + POOL_GROUP]
        for j in range(1 - half, win - half):
            s = s + ext[HALO + j:HALO + j + TM, c0:c0 + POOL_GROUP]
        lo = jnp.maximum(t_loc - half, 0)
        hi = jnp.minimum(t_loc + (win - half), n_seq)
        cnt = (hi - lo).astype(F32)
        pooled = s / cnt - h[:, c0:c0 + POOL_GROUP]
        mixed.append(_dot(pooled.astype(BF16), w_ref[g]))
    o = jnp.concatenate(mixed, axis=1) * ps_ref[...]
    y = ALPHA * x + gate_ref[0] * o
    o_ref[...] = _layer_norm(y, g_ref[...], b_ref[...])


def _pool_layer(x, shift, scale, gate, w_grp, pool_scale, ln_g, ln_b):
    prev, nxt = _halo_specs(D_MODEL)
    return pl.pallas_call(
        _pool_kernel,
        grid=(NT_ALL,),
        in_specs=[_row_spec(D_MODEL), prev, nxt, _mod_spec(0), _mod_spec(0), _mod_spec(0),
                  _full_spec((len(POOL_WINDOWS), POOL_GROUP, POOL_GROUP)), _full_spec((1, D_MODEL)),
                  _full_spec((1, D_MODEL)), _full_spec((1, D_MODEL))],
        out_specs=_row_spec(D_MODEL),
        out_shape=jax.ShapeDtypeStruct((T_ALL, D_MODEL), F32),
        scratch_shapes=[pltpu.VMEM((TM + 2 * HALO, D_MODEL), F32)],
        compiler_params=_params("arbitrary"),
        name="pool_layer",
    )(x, x, x, shift, scale, gate, w_grp, pool_scale, ln_g, ln_b)


PROJ_TN = 512


def _proj_kernel(*refs, norm, rope, scale):
    refs = list(refs)
    h_ref, w_ref = refs[:2]
    rest = refs[2:]
    gain_ref = rest.pop(0) if norm else None
    if rope:
        c_ref, a_ref, b_ref = rest[:3]
        rest = rest[3:]
    (o_ref,) = rest
    acc = _dot(h_ref[...], w_ref[...])
    for hh in range(PROJ_TN // HEAD_DIM):
        xh = acc[:, hh * HEAD_DIM:(hh + 1) * HEAD_DIM]
        if norm:
            ms = jnp.mean(xh * xh, axis=-1, keepdims=True)
            xh = xh * lax.rsqrt(ms + RMS_EPS) * gain_ref[...]
        if rope:
            xh = (xh * c_ref[...] + pltpu.roll(xh, HEAD_DIM - 32, axis=1) * a_ref[...]
                  + pltpu.roll(xh, 32, axis=1) * b_ref[...])
        if scale != 1.0:
            xh = xh * scale
        o_ref[:, hh * HEAD_DIM:(hh + 1) * HEAD_DIM] = xh.astype(o_ref.dtype)


def _project(h, w, col0, width, *, gain=None, rope_tabs=None, scale=1.0, tile0=0):
    nt = NT_ALL - tile0
    nb = width // PROJ_TN
    off = col0 // PROJ_TN
    in_specs = [pl.BlockSpec((TM, D_MODEL), lambda j, i: (i + tile0, 0)),
                pl.BlockSpec((D_MODEL, PROJ_TN), lambda j, i: (0, j + off))]
    args = [h, w]
    if gain is not None:
        in_specs.append(pl.BlockSpec((1, HEAD_DIM), lambda j, i: (0, 0)))
        args.append(gain)
    if rope_tabs is not None:
        in_specs += [pl.BlockSpec((TM, HEAD_DIM), lambda j, i: (i + tile0, 0))] * 3
        args += list(rope_tabs)
    kern = functools.partial(_proj_kernel, norm=gain is not None, rope=rope_tabs is not None, scale=scale)
    return pl.pallas_call(
        kern,
        grid=(nb, nt),
        in_specs=in_specs,
        out_specs=pl.BlockSpec((TM, PROJ_TN), lambda j, i: (i, j)),
        out_shape=jax.ShapeDtypeStruct((nt * TM, width), BF16),
        compiler_params=_params("arbitrary", "arbitrary"),
        name="project",
    )(*args)


def _softmax_step(s, v, m_ref, l_ref, acc_ref):
    m_prev = m_ref[...]
    m_new = jnp.maximum(m_prev, jnp.max(s, axis=-1, keepdims=True))
    p = jnp.exp(s - m_new)
    alpha = jnp.exp(m_prev - m_new)
    l_ref[...] = alpha * l_ref[...] + jnp.sum(p, axis=-1, keepdims=True)
    acc_ref[...] = alpha * acc_ref[...] + _dot(p.astype(BF16), v)
    m_ref[...] = m_new


def _qk(q, k):
    return lax.dot_general(q, k, (((1,), (1,)), ((), ())), preferred_element_type=F32)


GQA_PAIR = 2


def _gqa_kernel(q_ref, k_ref, v_ref, o_ref, m_ref, l_ref, acc_ref):
    qi = pl.program_id(1)
    q = jnp.concatenate([q_ref[:, hh * HEAD_DIM:(hh + 1) * HEAD_DIM] for hh in range(GQA_PAIR)], axis=0)
    m_ref[...] = jnp.full(m_ref.shape, -jnp.inf, F32)
    l_ref[...] = jnp.zeros(l_ref.shape, F32)
    acc_ref[...] = jnp.zeros(acc_ref.shape, F32)

    @pl.when(qi == 0)
    def _():
        _softmax_step(_qk(q, k_ref[0:CTX_LEN, :]), v_ref[0:CTX_LEN, :], m_ref, l_ref, acc_ref)

    @pl.when(qi > 0)
    def _():
        def body(j, carry):
            start = pl.multiple_of(j * KV_CHUNK, KV_CHUNK)
            _softmax_step(_qk(q, k_ref[pl.ds(start, KV_CHUNK), :]), v_ref[pl.ds(start, KV_CHUNK), :],
                          m_ref, l_ref, acc_ref)
            return carry
        lax.fori_loop(0, T_ALL // KV_CHUNK, body, 0)

    o = acc_ref[...] / l_ref[...]
    for hh in range(GQA_PAIR):
        o_ref[:, hh * HEAD_DIM:(hh + 1) * HEAD_DIM] = o[hh * TM:(hh + 1) * TM].astype(o_ref.dtype)


def _gqa_attention(q, k, v):
    pairs = GQA_GROUP // GQA_PAIR
    wq = GQA_PAIR * HEAD_DIM
    qspec = pl.BlockSpec((TM, wq), lambda kv, i, p: (i, kv * pairs + p))
    kvspec = pl.BlockSpec((T_ALL, HEAD_DIM), lambda kv, i, p: (0, kv))
    rows = GQA_PAIR * TM
    return pl.pallas_call(
        _gqa_kernel,
        grid=(GQA_KV_HEADS, NT_ALL, pairs),
        in_specs=[qspec, kvspec, kvspec],
        out_specs=qspec,
        out_shape=jax.ShapeDtypeStruct((T_ALL, D_MODEL), BF16),
        scratch_shapes=[pltpu.VMEM((rows, 1), F32), pltpu.VMEM((rows, 1), F32), pltpu.VMEM((rows, HEAD_DIM), F32)],
        compiler_params=_params("arbitrary", "arbitrary", "arbitrary"),
        name="gqa_attention",
    )(q, k, v)


DIFF_TQ = 512


def _diff_kernel(q_ref, k_ref, v_ref, lam_ref, sub_ref, o_ref, m0, l0, a0, m1, l1, a1, *, lam_init):
    for m_ref, l_ref, acc_ref in ((m0, l0, a0), (m1, l1, a1)):
        m_ref[...] = jnp.full(m_ref.shape, -jnp.inf, F32)
        l_ref[...] = jnp.zeros(l_ref.shape, F32)
        acc_ref[...] = jnp.zeros(acc_ref.shape, F32)
    q0 = q_ref[:, 0:HEAD_DIM]
    q1 = q_ref[:, HEAD_DIM:2 * HEAD_DIM]

    def body(j, carry):
        start = pl.multiple_of(j * KV_CHUNK, KV_CHUNK)
        kc = k_ref[pl.ds(start, KV_CHUNK), :]
        vc = v_ref[pl.ds(start, KV_CHUNK), :]
        _softmax_step(_qk(q0, kc[:, 0:HEAD_DIM]), vc, m0, l0, a0)
        _softmax_step(_qk(q1, kc[:, HEAD_DIM:2 * HEAD_DIM]), vc, m1, l1, a1)
        return carry
    lax.fori_loop(0, T_ALL // KV_CHUNK, body, 0)

    lv = lam_ref[...]
    lam = (jnp.exp(jnp.sum(lv[0:1] * lv[1:2], axis=-1, keepdims=True))
           - jnp.exp(jnp.sum(lv[2:3] * lv[3:4], axis=-1, keepdims=True)) + lam_init)
    o = a0[...] / l0[...] - lam * (a1[...] / l1[...])
    ms = jnp.mean(o * o, axis=-1, keepdims=True)
    o = o * lax.rsqrt(ms + RMS_EPS) * sub_ref[...] * (1.0 - lam_init)
    o_ref[...] = o.astype(o_ref.dtype)


def _diff_attention(q, k, v, lam_vecs, subln, lam_init):
    wv = 2 * HEAD_DIM
    qspec = pl.BlockSpec((DIFF_TQ, wv), lambda hd, i: (i, hd))
    kvspec = pl.BlockSpec((T_ALL, wv), lambda hd, i: (0, hd))
    stats = [pltpu.VMEM((DIFF_TQ, 1), F32), pltpu.VMEM((DIFF_TQ, 1), F32), pltpu.VMEM((DIFF_TQ, wv), F32)]
    return pl.pallas_call(
        functools.partial(_diff_kernel, lam_init=lam_init),
        grid=(DIFF_HEADS, SEQ // DIFF_TQ),
        in_specs=[qspec, kvspec, kvspec,
                  pl.BlockSpec((4, HEAD_DIM), lambda hd, i: (0, 0)), pl.BlockSpec((1, wv), lambda hd, i: (0, 0))],
        out_specs=qspec,
        out_shape=jax.ShapeDtypeStruct((SEQ, D_MODEL), BF16),
        scratch_shapes=stats + stats,
        compiler_params=_params("arbitrary", "arbitrary"),
        name="diff_attention",
    )(q, k, v, lam_vecs, subln)


def _lane_pick(vals, lane, idx):
    return jnp.sum(jnp.where(lane == idx, vals, 0.0), axis=-1, keepdims=True)


def _router_kernel(x_ref, sh_ref, sc_ref, w_ref, b_ref, meta_ref, cnt_ref, carry):
    i = pl.program_id(0)

    @pl.when(i == 0)
    def _():
        carry[...] = jnp.zeros(carry.shape, F32)

    h = x_ref[...] * (1.0 + sc_ref[0]) + sh_ref[0]
    logits = jnp.dot(h, w_ref[...], preferred_element_type=F32, precision=lax.Precision.HIGHEST) + b_ref[...]
    lane = lax.broadcasted_iota(jnp.int32, (TM, LANES), 1).astype(F32)
    neg = -jnp.inf
    big = float(LANES)

    grp = jnp.where(lane < N_GROUPS, logits, neg)
    gmax = jnp.max(grp, axis=-1, keepdims=True)
    gstar = jnp.min(jnp.where(grp == gmax, lane, big), axis=-1, keepdims=True)
    p_group = 1.0 / jnp.sum(jnp.exp(grp - gmax), axis=-1, keepdims=True)

    e_lo = N_GROUPS + EXPERTS_PER_GROUP * gstar
    own = jnp.logical_and(lane >= e_lo, lane < e_lo + EXPERTS_PER_GROUP)
    le = jnp.where(own, logits, neg)
    v1 = jnp.max(le, axis=-1, keepdims=True)
    i1 = jnp.min(jnp.where(le == v1, lane, big), axis=-1, keepdims=True)
    le2 = jnp.where(lane == i1, neg, le)
    v2 = jnp.max(le2, axis=-1, keepdims=True)
    i2 = jnp.min(jnp.where(le2 == v2, lane, big), axis=-1, keepdims=True)
    t2 = jnp.exp(v2 - v1)
    w1 = p_group / (1.0 + t2)
    w2 = p_group * t2 / (1.0 + t2)
    e1 = i1 - N_GROUPS
    e2 = i2 - N_GROUPS

    chosen = jnp.logical_or(lane == e1, lane == e2)
    onehot = jnp.where(chosen, 1.0, 0.0)
    r_io = lax.broadcasted_iota(jnp.int32, (TM, TM), 0)
    c_io = lax.broadcasted_iota(jnp.int32, (TM, TM), 1)
    tri = jnp.where(c_io < r_io, 1.0, 0.0).astype(BF16)
    before = carry[...] + _dot(tri, onehot.astype(BF16))
    r1 = _lane_pick(before, lane, e1)
    r2 = _lane_pick(before, lane, e2)
    carry[...] = carry[...] + jnp.sum(onehot, axis=0, keepdims=True)

    meta = jnp.where(lane == 0, e1, 0.0)
    for col, val in ((1, e2), (2, r1), (3, r2), (4, w1), (5, w2)):
        meta = jnp.where(lane == col, val, meta)
    meta_ref[...] = meta
    cnt_ref[...] = jnp.broadcast_to(carry[...], cnt_ref.shape)


def _router(x, shift, scale, w_r, b_r, *, kind0):
    nt = x.shape[0] // TM
    return pl.pallas_call(
        _router_kernel,
        grid=(nt,),
        in_specs=[_row_spec(D_MODEL), _mod_spec(kind0), _mod_spec(kind0),
                  _full_spec((D_MODEL, LANES)), _full_spec((1, LANES))],
        out_specs=[_row_spec(LANES), _full_spec((SUBLANES, LANES))],
        out_shape=[jax.ShapeDtypeStruct((nt * TM, LANES), F32), jax.ShapeDtypeStruct((SUBLANES, LANES), F32)],
        scratch_shapes=[pltpu.VMEM((1, LANES), F32)],
        compiler_params=_params("arbitrary"),
        name="router",
    )(x, shift, scale, w_r, b_r)


def _row_copy(src, src_row, dst, dst_row, sem):
    return pltpu.make_async_copy(src.at[pl.ds(src_row, 1)], dst.at[pl.ds(dst_row, 1)], sem)


def _dispatch_kernel(pos_ref, x_ref, sh_ref, sc_ref, init_ref, out_ref, hbuf, sem):
    del init_ref
    i = pl.program_id(0)
    hbuf[...] = x_ref[...] * (1.0 + sc_ref[0]) + sh_ref[0]

    def start(r, carry):
        for k in range(2):
            _row_copy(hbuf, r, out_ref, pos_ref[2 * (i * TM + r) + k], sem).start()
        return carry
    lax.fori_loop(0, TM, start, 0)

    def wait(r, carry):
        for k in range(2):
            _row_copy(hbuf, 0, out_ref, 0, sem).wait()
        return carry
    lax.fori_loop(0, TM, wait, 0)


def _dispatch(pos, x, shift, scale, n_sorted, *, kind0):
    nt = x.shape[0] // TM
    init = jnp.zeros((n_sorted, D_MODEL), F32)
    return pl.pallas_call(
        _dispatch_kernel,
        grid_spec=pltpu.PrefetchScalarGridSpec(
            num_scalar_prefetch=1,
            grid=(nt,),
            in_specs=[_row_spec(D_MODEL), _mod_spec(kind0), _mod_spec(kind0),
                      pl.BlockSpec(memory_space=pl.ANY)],
            out_specs=pl.BlockSpec(memory_space=pl.ANY),
            scratch_shapes=[pltpu.VMEM((TM, D_MODEL), F32), pltpu.SemaphoreType.DMA],
        ),
        out_shape=jax.ShapeDtypeStruct((n_sorted, D_MODEL), F32),
        input_output_aliases={4: 0},
        compiler_params=_params("arbitrary"),
        name="moe_dispatch",
    )(pos, x, shift, scale, init)


def _ffn_kernel(te_ref, nv_ref, a_ref, wg_ref, wu_ref, wd_ref, o_ref):
    i = pl.program_id(0)

    @pl.when(i < nv_ref[0])
    def _():
        a = a_ref[...].astype(BF16)
        g = _dot(a, wg_ref[0])
        u = _dot(a, wu_ref[0])
        act = (g * jax.nn.sigmoid(g) * u).astype(BF16)
        o_ref[...] = _dot(act, wd_ref[0])

    @pl.when(i >= nv_ref[0])
    def _():
        o_ref[...] = jnp.zeros(o_ref.shape, F32)


def _expert_ffn(tile_expert, n_valid, a_sorted, w_gate, w_up, w_down):
    n_tiles = a_sorted.shape[0] // TM
    return pl.pallas_call(
        _ffn_kernel,
        grid_spec=pltpu.PrefetchScalarGridSpec(
            num_scalar_prefetch=2,
            grid=(n_tiles,),
            in_specs=[pl.BlockSpec((TM, D_MODEL), lambda i, te, nv: (i, 0)),
                      pl.BlockSpec((1, D_MODEL, EXPERT_FF), lambda i, te, nv: (te[i], 0, 0)),
                      pl.BlockSpec((1, D_MODEL, EXPERT_FF), lambda i, te, nv: (te[i], 0, 0)),
                      pl.BlockSpec((1, EXPERT_FF, D_MODEL), lambda i, te, nv: (te[i], 0, 0))],
            out_specs=pl.BlockSpec((TM, D_MODEL), lambda i, te, nv: (i, 0)),
        ),
        out_shape=jax.ShapeDtypeStruct(a_sorted.shape, F32),
        compiler_params=_params("arbitrary"),
        name="moe_experts",
    )(tile_expert, n_valid, a_sorted, w_gate, w_up, w_down)


def _combine_kernel(pos_ref, y_ref, x_ref, meta_ref, gate_ref, g_ref, b_ref, *rest, emit_h):
    if emit_h:
        nsh_ref, nsc_ref, o_ref, h_ref, buf0, buf1, sem = rest
    else:
        o_ref, buf0, buf1, sem = rest
    i = pl.program_id(0)

    def start(r, carry):
        for k, buf in enumerate((buf0, buf1)):
            _row_copy(y_ref, pos_ref[2 * (i * TM + r) + k], buf, r, sem).start()
        return carry
    lax.fori_loop(0, TM, start, 0)

    def wait(r, carry):
        for buf in (buf0, buf1):
            _row_copy(y_ref, 0, buf, 0, sem).wait()
        return carry
    lax.fori_loop(0, TM, wait, 0)

    meta = meta_ref[...]
    moe = meta[:, 4:5] * buf0[...] + meta[:, 5:6] * buf1[...]
    y = ALPHA * x_ref[...] + gate_ref[0] * moe
    out = _layer_norm(y, g_ref[...], b_ref[...])
    o_ref[...] = out
    if emit_h:
        h_ref[...] = (out * (1.0 + nsc_ref[0]) + nsh_ref[0]).astype(h_ref.dtype)


def _combine(pos, y_sorted, x, meta, gate, ln_g, ln_b, next_mod, *, kind0):
    nt = x.shape[0] // TM
    emit_h = next_mod is not None
    mspec = _mod_spec(kind0)
    vec = _full_spec((1, D_MODEL))
    ospec = _row_spec(D_MODEL)
    in_specs = [pl.BlockSpec(memory_space=pl.ANY), ospec, _row_spec(LANES), mspec, vec, vec]
    args = [pos, y_sorted, x, meta, gate, ln_g, ln_b]
    out_specs = [ospec]
    out_shape = [jax.ShapeDtypeStruct((nt * TM, D_MODEL), F32)]
    if emit_h:
        in_specs += [mspec, mspec]
        args += list(next_mod)
        out_specs.append(ospec)
        out_shape.append(jax.ShapeDtypeStruct((nt * TM, D_MODEL), BF16))
    res = pl.pallas_call(
        functools.partial(_combine_kernel, emit_h=emit_h),
        grid_spec=pltpu.PrefetchScalarGridSpec(
            num_scalar_prefetch=1,
            grid=(nt,),
            in_specs=in_specs,
            out_specs=out_specs,
            scratch_shapes=[pltpu.VMEM((TM, D_MODEL), F32), pltpu.VMEM((TM, D_MODEL), F32),
                            pltpu.SemaphoreType.DMA],
        ),
        out_shape=out_shape,
        compiler_params=_params("arbitrary"),
        name="moe_combine",
    )(*args)
    return res if emit_h else (res[0], None)


def _moe_layer(x, mods, w_r, b_r, w_gate, w_up, w_down, ln_g, ln_b, next_mod, *, kind0):
    shift, scale, gate = mods
    n_tok = x.shape[0]
    n_tiles = -(-(2 * n_tok + N_EXPERTS * (TM - 1)) // TM)
    meta, cnt = _router(x, shift, scale, w_r, b_r, kind0=kind0)

    counts = cnt[0, :N_EXPERTS].astype(jnp.int32)
    padded = ((counts + TM - 1) // TM) * TM
    ends = jnp.cumsum(padded)
    offs = ends - padded
    experts = meta[:, 0:2].astype(jnp.int32)
    ranks = meta[:, 2:4].astype(jnp.int32)
    pos = (offs[experts] + ranks).reshape(-1)
    n_valid = ends[-1] // TM
    tile_start = jnp.arange(n_tiles, dtype=jnp.int32) * TM
    te = jnp.sum(tile_start[:, None] >= ends[None, :], axis=1).astype(jnp.int32)
    te_last = jnp.sum((n_valid - 1) * TM >= ends).astype(jnp.int32)
    te = jnp.where(tile_start < ends[-1], te, te_last)

    a_sorted = _dispatch(pos, x, shift, scale, n_tiles * TM, kind0=kind0)
    y_sorted = _expert_ffn(te, n_valid.reshape(1).astype(jnp.int32), a_sorted, w_gate, w_up, w_down)
    return _combine(pos, y_sorted, x, meta, gate, ln_g, ln_b, next_mod, kind0=kind0)


def _rope_tables():
    rows = SEQ // GRID_W
    row = jnp.repeat(jnp.arange(rows, dtype=F32), GRID_W)
    col = jnp.tile(jnp.arange(GRID_W, dtype=F32), rows)
    axis_dim = HEAD_DIM // 2
    inv = ROPE_THETA ** (-jnp.arange(0, axis_dim, 2, dtype=F32) / axis_dim)
    cr, sr = jnp.cos(row[:, None] * inv), jnp.sin(row[:, None] * inv)
    cc, sc = jnp.cos(col[:, None] * inv), jnp.sin(col[:, None] * inv)
    z = jnp.zeros_like(sr)
    c_tab = jnp.concatenate([cr, cr, cc, cc], axis=1)
    a_tab = jnp.concatenate([-sr, z, -sc, z], axis=1)
    b_tab = jnp.concatenate([z, sr, z, sc], axis=1)
    ctx = (jnp.ones((CTX_LEN, HEAD_DIM), F32), jnp.zeros((CTX_LEN, HEAD_DIM), F32), jnp.zeros((CTX_LEN, HEAD_DIM), F32))
    return tuple(jnp.concatenate([c0, t], axis=0) for c0, t in zip(ctx, (c_tab, a_tab, b_tab)))


def kernel(x, c, ctx, c_ctx, mod_w, mod_b, ln_g, ln_b, rt_grp_w, rt_grp_b, rt_exp_w, rt_exp_b,
           ex_w_gate, ex_w_up, ex_w_down, conv_in_w, conv_w, conv_out_w, pool_w, pool_scale,
           gqa_qkv_w, gqa_qk_norm, gqa_out_w, diff_qkv_w, diff_lambda, diff_subln, diff_out_w):
    assert x.shape == (1, SEQ, D_MODEL) and ctx.shape == (1, CTX_LEN, D_MODEL)
    xs = jnp.concatenate([ctx[0], x[0]], axis=0)

    c_cols = jnp.zeros((D_MODEL, LANES), F32).at[:, 0].set(c_ctx).at[:, 1].set(c[0])
    mods = _modulation(c_cols, mod_w, mod_b)
    mods = mods[:, :2].reshape(DEPTH, 2, N_MOD, 1, D_MODEL)
    mod = lambda layer, m: mods[layer, :, m]

    w_r = jnp.zeros((DEPTH, D_MODEL, LANES), F32)
    w_r = w_r.at[:, :, :N_GROUPS].set(rt_grp_w).at[:, :, N_GROUPS:N_GROUPS + N_EXPERTS].set(rt_exp_w)
    b_r = jnp.zeros((DEPTH, 1, LANES), F32)
    b_r = b_r.at[:, 0, :N_GROUPS].set(rt_grp_b).at[:, 0, N_GROUPS:N_GROUPS + N_EXPERTS].set(rt_exp_b)
    rope = _rope_tables()
    sm_scale = HEAD_DIM ** -0.5
    bf = lambda w: w.astype(BF16)

    def moe(layer, x_in, next_mod, kind0=0):
        return _moe_layer(x_in, (mod(layer, 3), mod(layer, 4), mod(layer, 5)), w_r[layer], b_r[layer],
                          bf(ex_w_gate[layer]), bf(ex_w_up[layer]), bf(ex_w_down[layer]),
                          ln_g[layer, 1:2], ln_b[layer, 1:2], next_mod, kind0=kind0)

    h = _modulate(xs, mod(0, 0), mod(0, 1))
    gb, p = _conv_in(h, bf(conv_in_w[0]))
    r = _conv_mix(p, gb, conv_w[0])
    xs = _mixer_out(r, bf(conv_out_w[0]), xs, mod(0, 2), ln_g[0, 0:1], ln_b[0, 0:1], kind0=0)
    xs, _ = moe(0, xs, None)

    xs = _pool_layer(xs, mod(1, 0), mod(1, 1), mod(1, 2), bf(pool_w[0]), pool_scale[0:1],
                     ln_g[1, 0:1], ln_b[1, 0:1])
    xs, h = moe(1, xs, (mod(2, 0), mod(2, 1)))

    wqkv = bf(gqa_qkv_w[0])
    nq, nkv = GQA_HEADS * HEAD_DIM, GQA_KV_HEADS * HEAD_DIM
    q = _project(h, wqkv, 0, nq, gain=gqa_qk_norm[0, 0:1], rope_tabs=rope, scale=sm_scale)
    k = _project(h, wqkv, nq, nkv, gain=gqa_qk_norm[0, 1:2], rope_tabs=rope)
    v = _project(h, wqkv, nq + nkv, nkv)
    o = _gqa_attention(q, k, v)
    xs = _mixer_out(o, bf(gqa_out_w[0]), xs, mod(2, 2), ln_g[2, 0:1], ln_b[2, 0:1], kind0=0)
    xs, h = moe(2, xs, (mod(3, 0), mod(3, 1)))

    lam_init = 0.8 - 0.6 * math.exp(-0.3 * 3)
    wqkv = bf(diff_qkv_w[0])
    q = _project(h, wqkv, 0, D_MODEL, rope_tabs=rope, scale=sm_scale, tile0=1)
    k = _project(h, wqkv, D_MODEL, D_MODEL, rope_tabs=rope)
    v = _project(h, wqkv, 2 * D_MODEL, D_MODEL)
    o = _diff_attention(q, k, v, diff_lambda[0], diff_subln[0:1], lam_init)
    xs = _mixer_out(o, bf(diff_out_w[0]), xs, mod(3, 2), ln_g[3, 0:1], ln_b[3, 0:1], kind0=1)
    out, _ = moe(3, xs, None, kind0=1)
    return out[None]
```

```python
import functools
import math

import jax
import jax.numpy as jnp
from jax import lax
from jax.experimental import pallas as pl
from jax.experimental.pallas import tpu as pltpu

D_MODEL = 2048
SEQ = 8192
DEPTH = 4
GRID_W = 64
CTX_LEN = 256
ALPHA = (2.0 * DEPTH) ** 0.25
ROPE_THETA = 10000.0
LN_EPS = 1e-6
RMS_EPS = 1e-6
N_MOD = 6
HEAD_DIM = 128
GQA_HEADS = D_MODEL // HEAD_DIM
GQA_KV_HEADS = GQA_HEADS // 4
GQA_GROUP = GQA_HEADS // GQA_KV_HEADS
DIFF_HEADS = D_MODEL // (2 * HEAD_DIM)
POOL_WINDOWS = (2, 4, 8, 16)
POOL_GROUP = D_MODEL // len(POOL_WINDOWS)
N_GROUPS = 4
EXPERTS_PER_GROUP = 4
N_EXPERTS = N_GROUPS * EXPERTS_PER_GROUP
EXPERT_FF = 3 * D_MODEL // 8

LANES = 128
SUBLANES = 8
TM = 256
T_ALL = CTX_LEN + SEQ
NT_ALL = T_ALL // TM
HALO = SUBLANES
VMEM_LIMIT = 56 * 1024 * 1024

assert CTX_LEN == TM and SEQ % TM == 0 and DEPTH == 4

F32 = jnp.float32
BF16 = jnp.bfloat16


def _params(*sem):
    return pltpu.CompilerParams(dimension_semantics=sem, vmem_limit_bytes=VMEM_LIMIT)


def _dot(a, b):
    return jnp.dot(a, b, preferred_element_type=F32)


def _layer_norm(y, g, b):
    mu = jnp.mean(y, axis=-1, keepdims=True)
    d = y - mu
    var = jnp.mean(d * d, axis=-1, keepdims=True)
    return d * lax.rsqrt(var + LN_EPS) * g + b


def _mod_spec(kind0):
    return pl.BlockSpec((1, 1, D_MODEL), lambda i, *_: (jnp.minimum(i + kind0, 1), 0, 0))


def _row_spec(width, tile0=0):
    return pl.BlockSpec((TM, width), lambda i, *_: (i + tile0, 0))


def _full_spec(shape):
    nd = len(shape)
    return pl.BlockSpec(shape, lambda i, *_: (0,) * nd)


MOD_TN = 512


def _mod_kernel(c_ref, w_ref, b_ref, o_ref):
    w = w_ref[0]
    rows = []
    for r in range(2):
        c = c_ref[:, r:r + 1]
        s = c * jax.nn.sigmoid(c)
        rows.append(jnp.sum(w * s, axis=0, keepdims=True))
    rows.append(jnp.zeros((SUBLANES - 2, MOD_TN), F32))
    o_ref[0] = jnp.concatenate(rows, axis=0) + b_ref[0]


def _modulation(c_cols, mod_w, mod_b):
    n_out = N_MOD * D_MODEL
    return pl.pallas_call(
        _mod_kernel,
        grid=(DEPTH, n_out // MOD_TN),
        in_specs=[
            pl.BlockSpec((D_MODEL, LANES), lambda l, j: (0, 0)),
            pl.BlockSpec((1, D_MODEL, MOD_TN), lambda l, j: (l, 0, j)),
            pl.BlockSpec((1, 1, MOD_TN), lambda l, j: (l, 0, j)),
        ],
        out_specs=pl.BlockSpec((1, SUBLANES, MOD_TN), lambda l, j: (l, 0, j)),
        out_shape=jax.ShapeDtypeStruct((DEPTH, SUBLANES, n_out), F32),
        compiler_params=_params("arbitrary", "arbitrary"),
        name="modulation",
    )(c_cols, mod_w, mod_b.reshape(DEPTH, 1, n_out))


def _modulate_kernel(x_ref, sh_ref, sc_ref, o_ref):
    o_ref[...] = (x_ref[...] * (1.0 + sc_ref[0]) + sh_ref[0]).astype(o_ref.dtype)


def _modulate(x, shift, scale):
    return pl.pallas_call(
        _modulate_kernel,
        grid=(NT_ALL,),
        in_specs=[_row_spec(D_MODEL), _mod_spec(0), _mod_spec(0)],
        out_specs=_row_spec(D_MODEL),
        out_shape=jax.ShapeDtypeStruct((T_ALL, D_MODEL), BF16),
        compiler_params=_params("arbitrary"),
        name="modulate",
    )(x, shift, scale)


CONV_TN = 512


def _conv_in_kernel(h_ref, wb_ref, wc_ref, wv_ref, gb_ref, p_ref):
    h = h_ref[...]
    gb_ref[...] = _dot(h, wb_ref[...])
    p_ref[...] = _dot(h, wc_ref[...]) * _dot(h, wv_ref[...])


def _conv_in(h, w_in):
    nb = D_MODEL // CONV_TN
    wspec = lambda off: pl.BlockSpec((D_MODEL, CONV_TN), lambda j, i: (0, j + off * nb))
    ospec = pl.BlockSpec((TM, CONV_TN), lambda j, i: (i, j))
    return pl.pallas_call(
        _conv_in_kernel,
        grid=(nb, NT_ALL),
        in_specs=[pl.BlockSpec((TM, D_MODEL), lambda j, i: (i, 0)), wspec(0), wspec(1), wspec(2)],
        out_specs=[ospec, ospec],
        out_shape=[jax.ShapeDtypeStruct((T_ALL, D_MODEL), F32)] * 2,
        compiler_params=_params("arbitrary", "arbitrary"),
        name="conv_in",
    )(h, w_in, w_in, w_in)


def _halo_flags(i, nt):
    has_prev = (i >= 2).astype(F32)
    has_next = jnp.logical_and(i >= 1, i < nt - 1).astype(F32)
    return has_prev, has_next


def _conv_mix_kernel(p_ref, pp_ref, pn_ref, gb_ref, cw_ref, r_ref, ext):
    has_prev, has_next = _halo_flags(pl.program_id(0), NT_ALL)
    ext[0:HALO] = pp_ref[...] * has_prev
    ext[HALO:HALO + TM] = p_ref[...]
    ext[HALO + TM:] = pn_ref[...] * has_next
    q = (ext[HALO - 1:HALO - 1 + TM] * cw_ref[0:1]
         + ext[HALO:HALO + TM] * cw_ref[1:2]
         + ext[HALO + 1:HALO + 1 + TM] * cw_ref[2:3])
    r_ref[...] = (gb_ref[...] * q).astype(r_ref.dtype)


def _halo_specs(width):
    per = TM // HALO
    last = T_ALL // HALO - 1
    prev = pl.BlockSpec((HALO, width), lambda i: (jnp.maximum(i * per - 1, 0), 0))
    nxt = pl.BlockSpec((HALO, width), lambda i: (jnp.minimum((i + 1) * per, last), 0))
    return prev, nxt


def _conv_mix(p, gb, conv_w):
    prev, nxt = _halo_specs(D_MODEL)
    return pl.pallas_call(
        _conv_mix_kernel,
        grid=(NT_ALL,),
        in_specs=[_row_spec(D_MODEL), prev, nxt, _row_spec(D_MODEL), _full_spec((3, D_MODEL))],
        out_specs=_row_spec(D_MODEL),
        out_shape=jax.ShapeDtypeStruct((T_ALL, D_MODEL), BF16),
        scratch_shapes=[pltpu.VMEM((TM + 2 * HALO, D_MODEL), F32)],
        compiler_params=_params("arbitrary"),
        name="conv_mix",
    )(p, p, p, gb, conv_w)


def _mixer_out_kernel(r_ref, w_ref, x_ref, gate_ref, g_ref, b_ref, o_ref):
    o = _dot(r_ref[...], w_ref[...])
    y = ALPHA * x_ref[...] + gate_ref[0] * o
    o_ref[...] = _layer_norm(y, g_ref[...], b_ref[...])


def _mixer_out(r, w_out, x, gate, ln_g, ln_b, *, kind0):
    nt = r.shape[0] // TM
    return pl.pallas_call(
        _mixer_out_kernel,
        grid=(nt,),
        in_specs=[_row_spec(D_MODEL), _full_spec((D_MODEL, D_MODEL)), _row_spec(D_MODEL, NT_ALL - nt),
                  _mod_spec(kind0), _full_spec((1, D_MODEL)), _full_spec((1, D_MODEL))],
        out_specs=_row_spec(D_MODEL),
        out_shape=jax.ShapeDtypeStruct((nt * TM, D_MODEL), F32),
        compiler_params=_params("arbitrary"),
        name="mixer_out",
    )(r, w_out, x, gate, ln_g, ln_b)


def _pool_kernel(x_ref, xp_ref, xn_ref, sh_ref, sc_ref, gate_ref, w_ref, ps_ref, g_ref, b_ref, o_ref, ext):
    i = pl.program_id(0)
    has_prev, has_next = _halo_flags(i, NT_ALL)
    one_sc = 1.0 + sc_ref[0]
    sh = sh_ref[0]
    x = x_ref[...]
    h = x * one_sc + sh
    ext[0:HALO] = (xp_ref[...] * one_sc + sh) * has_prev
    ext[HALO:HALO + TM] = h
    ext[HALO + TM:] = (xn_ref[...] * one_sc + sh) * has_next
    t_loc = lax.broadcasted_iota(jnp.int32, (TM, 1), 0) + jnp.maximum(i - 1, 0) * TM
    n_seq = jnp.where(i == 0, CTX_LEN, SEQ)
    mixed = []
    for g, win in enumerate(POOL_WINDOWS):
        c0 = g * POOL_GROUP
        half = win // 2
        s = ext[HALO - half:HALO - half + TM, c0:c0 + POOL_GROUP]
        for j in range(1 - half, win - half):
            s = s + ext[HALO + j:HALO + j + TM, c0:c0 + POOL_GROUP]
        lo = jnp.maximum(t_loc - half, 0)
        hi = jnp.minimum(t_loc + (win - half), n_seq)
        cnt = (hi - lo).astype(F32)
        pooled = s / cnt - h[:, c0:c0 + POOL_GROUP]
        mixed.append(_dot(pooled.astype(BF16), w_ref[g]))
    o = jnp.concatenate(mixed, axis=1) * ps_ref[...]
    y = ALPHA * x + gate_ref[0] * o
    o_ref[...] = _layer_norm(y, g_ref[...], b_ref[...])


def _pool_layer(x, shift, scale, gate, w_grp, pool_scale, ln_g, ln_b):
    prev, nxt = _halo_specs(D_MODEL)
    return pl.pallas_call(
        _pool_kernel,
        grid=(NT_ALL,),
        in_specs=[_row_spec(D_MODEL), prev, nxt, _mod_spec(0), _mod_spec(0), _mod_spec(0),
                  _full_spec((len(POOL_WINDOWS), POOL_GROUP, POOL_GROUP)), _full_spec((1, D_MODEL)),
                  _full_spec((1, D_MODEL)), _full_spec((1, D_MODEL))],
        out_specs=_row_spec(D_MODEL),
        out_shape=jax.ShapeDtypeStruct((T_ALL, D_MODEL), F32),
        scratch_shapes=[pltpu.VMEM((TM + 2 * HALO, D_MODEL), F32)],
        compiler_params=_params("arbitrary"),
        name="pool_layer",
    )(x, x, x, shift, scale, gate, w_grp, pool_scale, ln_g, ln_b)


PROJ_TN = 512


def _proj_kernel(*refs, norm, rope, scale, chunk_width):
    refs = list(refs)
    h_ref, w_ref = refs[:2]
    rest = refs[2:]
    gain_ref = rest.pop(0) if norm else None
    if rope:
        c_ref, a_ref, b_ref = rest[:3]
        rest = rest[3:]
    (o_ref,) = rest
    acc = _dot(h_ref[...], w_ref[...])
    if chunk_width:
        for hh in range(PROJ_TN // chunk_width):
            o_ref[hh, 0] = acc[:, hh * chunk_width:(hh + 1) * chunk_width].T.astype(o_ref.dtype)
        return
    for hh in range(PROJ_TN // HEAD_DIM):
        xh = acc[:, hh * HEAD_DIM:(hh + 1) * HEAD_DIM]
        if norm:
            ms = jnp.mean(xh * xh, axis=-1, keepdims=True)
            xh = xh * lax.rsqrt(ms + RMS_EPS) * gain_ref[...]
        if rope:
            xh = (xh * c_ref[...] + pltpu.roll(xh, HEAD_DIM - 32, axis=1) * a_ref[...]
                  + pltpu.roll(xh, 32, axis=1) * b_ref[...])
        if scale != 1.0:
            o_ref[hh * HEAD_DIM:(hh + 1) * HEAD_DIM, :] = (xh * scale).T.astype(o_ref.dtype)
        else:
            o_ref[:, hh * HEAD_DIM:(hh + 1) * HEAD_DIM] = xh.astype(o_ref.dtype)


def _project(h, w, col0, width, *, gain=None, rope_tabs=None, scale=1.0, tile0=0, chunk_width=0):
    nt = NT_ALL - tile0
    nb = width // PROJ_TN
    off = col0 // PROJ_TN
    in_specs = [pl.BlockSpec((TM, D_MODEL), lambda j, i: (i + tile0, 0)),
                pl.BlockSpec((D_MODEL, PROJ_TN), lambda j, i: (0, j + off))]
    args = [h, w]
    if gain is not None:
        in_specs.append(pl.BlockSpec((1, HEAD_DIM), lambda j, i: (0, 0)))
        args.append(gain)
    if rope_tabs is not None:
        in_specs += [pl.BlockSpec((TM, HEAD_DIM), lambda j, i: (i + tile0, 0))] * 3
        args += list(rope_tabs)
    if chunk_width:
        per = PROJ_TN // chunk_width
        out_spec = pl.BlockSpec((per, 1, chunk_width, TM), lambda j, i: (j, i, 0, 0))
        out_shape = jax.ShapeDtypeStruct((width // chunk_width, nt, chunk_width, TM), BF16)
    elif scale != 1.0:
        out_spec = pl.BlockSpec((PROJ_TN, TM), lambda j, i: (j, i))
        out_shape = jax.ShapeDtypeStruct((width, nt * TM), BF16)
    else:
        out_spec = pl.BlockSpec((TM, PROJ_TN), lambda j, i: (i, j))
        out_shape = jax.ShapeDtypeStruct((nt * TM, width), BF16)
    kern = functools.partial(_proj_kernel, norm=gain is not None, rope=rope_tabs is not None, scale=scale,
                             chunk_width=chunk_width)
    return pl.pallas_call(
        kern,
        grid=(nb, nt),
        in_specs=in_specs,
        out_specs=out_spec,
        out_shape=out_shape,
        compiler_params=_params("arbitrary", "arbitrary"),
        name="project",
    )(*args)


def _softmax_update(s, vt_c, m_ref, l_ref, acc_ref):
    m_prev = m_ref[...]
    m_new = jnp.maximum(m_prev, jnp.max(s, axis=0, keepdims=True))
    p = jnp.exp2(s - m_new)
    alpha = jnp.exp2(m_prev - m_new)
    l_ref[...] = alpha * l_ref[...] + jnp.sum(p, axis=0, keepdims=True)
    acc_ref[...] = alpha * acc_ref[...] + _dot(vt_c, p.astype(BF16))
    m_ref[...] = m_new


KEY_TILES_PER_ITER = 3


def _key_tile(k_ref, j):
    if isinstance(j, int):
        return k_ref[j * TM:(j + 1) * TM, :]
    return k_ref[pl.ds(pl.multiple_of(j * TM, TM), TM), :]


def _softmax_pipeline(steps, m_ref, l_ref, acc_ref):
    s_next = steps[0][0]()
    for i, (_, vt_c, stream) in enumerate(steps):
        s_cur = s_next
        if i + 1 < len(steps):
            s_next = steps[i + 1][0]()
        _softmax_update(s_cur, vt_c, m_ref.at[stream], l_ref.at[stream], acc_ref.at[stream])


def _attend(make_steps, n_streams, context_queries, m_ref, l_ref, acc_ref):
    for s in range(n_streams):
        m_ref[s] = jnp.full(m_ref.shape[1:], -jnp.inf, F32)
        l_ref[s] = jnp.zeros(l_ref.shape[1:], F32)
        acc_ref[s] = jnp.zeros(acc_ref.shape[1:], F32)

    if context_queries is not None:
        @pl.when(context_queries)
        def _():
            _softmax_pipeline(make_steps(0), m_ref, l_ref, acc_ref)

    def body(it, carry):
        steps = []
        for u in range(KEY_TILES_PER_ITER):
            steps += make_steps(it * KEY_TILES_PER_ITER + u)
        _softmax_pipeline(steps, m_ref, l_ref, acc_ref)
        return carry

    def all_tiles():
        assert NT_ALL % KEY_TILES_PER_ITER == 0
        lax.fori_loop(0, NT_ALL // KEY_TILES_PER_ITER, body, 0)

    if context_queries is None:
        all_tiles()
    else:
        pl.when(jnp.logical_not(context_queries))(all_tiles)


def _gqa_kernel(qt_ref, k_ref, vt_ref, o_ref, m_ref, l_ref, acc_ref):
    def make_steps(j):
        k_c = _key_tile(k_ref, j)
        vt_c = vt_ref[0, j]
        return [(functools.partial(lambda g: _dot(k_c, qt_ref[g * HEAD_DIM:(g + 1) * HEAD_DIM, :]), g), vt_c, g)
                for g in range(GQA_GROUP)]
    _attend(make_steps, GQA_GROUP, pl.program_id(1) == 0, m_ref, l_ref, acc_ref)

    for g in range(GQA_GROUP):
        o = acc_ref[g] / l_ref[g]
        o_ref[:, g * HEAD_DIM:(g + 1) * HEAD_DIM] = o.T.astype(o_ref.dtype)


def _gqa_attention(qt, k, vt):
    wq = GQA_GROUP * HEAD_DIM
    return pl.pallas_call(
        _gqa_kernel,
        grid=(GQA_KV_HEADS, NT_ALL),
        in_specs=[pl.BlockSpec((wq, TM), lambda kv, i: (kv, i)),
                  pl.BlockSpec((T_ALL, HEAD_DIM), lambda kv, i: (0, kv)),
                  pl.BlockSpec((1, NT_ALL, HEAD_DIM, TM), lambda kv, i: (kv, 0, 0, 0))],
        out_specs=pl.BlockSpec((TM, wq), lambda kv, i: (i, kv)),
        out_shape=jax.ShapeDtypeStruct((T_ALL, D_MODEL), BF16),
        scratch_shapes=[pltpu.VMEM((GQA_GROUP, 1, TM), F32), pltpu.VMEM((GQA_GROUP, 1, TM), F32),
                        pltpu.VMEM((GQA_GROUP, HEAD_DIM, TM), F32)],
        compiler_params=_params("arbitrary", "arbitrary"),
        name="gqa_attention",
    )(qt, k, vt)


def _diff_kernel(qt_ref, k_ref, vt_ref, lam_ref, sub_ref, o_ref, m_ref, l_ref, acc_ref, *, lam_init):
    def make_steps(j):
        k_c = _key_tile(k_ref, j)
        vt_c = vt_ref[0, j]
        return [(functools.partial(lambda s: _dot(k_c[:, s * HEAD_DIM:(s + 1) * HEAD_DIM],
                                                  qt_ref[s * HEAD_DIM:(s + 1) * HEAD_DIM, :]), s), vt_c, s)
                for s in range(2)]
    _attend(make_steps, 2, None, m_ref, l_ref, acc_ref)

    lv = lam_ref[...]
    lam = (jnp.exp(jnp.sum(lv[0:1] * lv[1:2], axis=-1, keepdims=True))
           - jnp.exp(jnp.sum(lv[2:3] * lv[3:4], axis=-1, keepdims=True)) + lam_init)
    o = (acc_ref[0] / l_ref[0] - lam * (acc_ref[1] / l_ref[1])).T
    ms = jnp.mean(o * o, axis=-1, keepdims=True)
    o = o * lax.rsqrt(ms + RMS_EPS) * sub_ref[...] * (1.0 - lam_init)
    o_ref[...] = o.astype(o_ref.dtype)


def _diff_attention(qt, k, vt, lam_vecs, subln, lam_init):
    wv = 2 * HEAD_DIM
    return pl.pallas_call(
        functools.partial(_diff_kernel, lam_init=lam_init),
        grid=(DIFF_HEADS, SEQ // TM),
        in_specs=[pl.BlockSpec((wv, TM), lambda hd, i: (hd, i)),
                  pl.BlockSpec((T_ALL, wv), lambda hd, i: (0, hd)),
                  pl.BlockSpec((1, NT_ALL, wv, TM), lambda hd, i: (hd, 0, 0, 0)),
                  pl.BlockSpec((4, HEAD_DIM), lambda hd, i: (0, 0)), pl.BlockSpec((1, wv), lambda hd, i: (0, 0))],
        out_specs=pl.BlockSpec((TM, wv), lambda hd, i: (i, hd)),
        out_shape=jax.ShapeDtypeStruct((SEQ, D_MODEL), BF16),
        scratch_shapes=[pltpu.VMEM((2, 1, TM), F32), pltpu.VMEM((2, 1, TM), F32), pltpu.VMEM((2, wv, TM), F32)],
        compiler_params=_params("arbitrary", "arbitrary"),
        name="diff_attention",
    )(qt, k, vt, lam_vecs, subln)


def _lane_pick(vals, lane, idx):
    return jnp.sum(jnp.where(lane == idx, vals, 0.0), axis=-1, keepdims=True)


def _router_kernel(x_ref, sh_ref, sc_ref, w_ref, b_ref, meta_ref, cnt_ref, carry):
    i = pl.program_id(0)

    @pl.when(i == 0)
    def _():
        carry[...] = jnp.zeros(carry.shape, F32)

    h = x_ref[...] * (1.0 + sc_ref[0]) + sh_ref[0]
    logits = jnp.dot(h, w_ref[...], preferred_element_type=F32, precision=lax.Precision.HIGHEST) + b_ref[...]
    lane = lax.broadcasted_iota(jnp.int32, (TM, LANES), 1).astype(F32)
    neg = -jnp.inf
    big = float(LANES)

    grp = jnp.where(lane < N_GROUPS, logits, neg)
    gmax = jnp.max(grp, axis=-1, keepdims=True)
    gstar = jnp.min(jnp.where(grp == gmax, lane, big), axis=-1, keepdims=True)
    p_group = 1.0 / jnp.sum(jnp.exp(grp - gmax), axis=-1, keepdims=True)

    e_lo = N_GROUPS + EXPERTS_PER_GROUP * gstar
    own = jnp.logical_and(lane >= e_lo, lane < e_lo + EXPERTS_PER_GROUP)
    le = jnp.where(own, logits, neg)
    v1 = jnp.max(le, axis=-1, keepdims=True)
    i1 = jnp.min(jnp.where(le == v1, lane, big), axis=-1, keepdims=True)
    le2 = jnp.where(lane == i1, neg, le)
    v2 = jnp.max(le2, axis=-1, keepdims=True)
    i2 = jnp.min(jnp.where(le2 == v2, lane, big), axis=-1, keepdims=True)
    t2 = jnp.exp(v2 - v1)
    w1 = p_group / (1.0 + t2)
    w2 = p_group * t2 / (1.0 + t2)
    e1 = i1 - N_GROUPS
    e2 = i2 - N_GROUPS

    chosen = jnp.logical_or(lane == e1, lane == e2)
    onehot = jnp.where(chosen, 1.0, 0.0)
    r_io = lax.broadcasted_iota(jnp.int32, (TM, TM), 0)
    c_io = lax.broadcasted_iota(jnp.int32, (TM, TM), 1)
    tri = jnp.where(c_io < r_io, 1.0, 0.0).astype(BF16)
    before = carry[...] + _dot(tri, onehot.astype(BF16))
    r1 = _lane_pick(before, lane, e1)
    r2 = _lane_pick(before, lane, e2)
    carry[...] = carry[...] + jnp.sum(onehot, axis=0, keepdims=True)

    meta = jnp.where(lane == 0, e1, 0.0)
    for col, val in ((1, e2), (2, r1), (3, r2), (4, w1), (5, w2)):
        meta = jnp.where(lane == col, val, meta)
    meta_ref[...] = meta
    cnt_ref[...] = jnp.broadcast_to(carry[...], cnt_ref.shape)


def _router(x, shift, scale, w_r, b_r, *, kind0):
    nt = x.shape[0] // TM
    return pl.pallas_call(
        _router_kernel,
        grid=(nt,),
        in_specs=[_row_spec(D_MODEL), _mod_spec(kind0), _mod_spec(kind0),
                  _full_spec((D_MODEL, LANES)), _full_spec((1, LANES))],
        out_specs=[_row_spec(LANES), _full_spec((SUBLANES, LANES))],
        out_shape=[jax.ShapeDtypeStruct((nt * TM, LANES), F32), jax.ShapeDtypeStruct((SUBLANES, LANES), F32)],
        scratch_shapes=[pltpu.VMEM((1, LANES), F32)],
        compiler_params=_params("arbitrary"),
        name="router",
    )(x, shift, scale, w_r, b_r)


def _row_copy(src, src_row, dst, dst_row, sem):
    return pltpu.make_async_copy(src.at[pl.ds(src_row, 1)], dst.at[pl.ds(dst_row, 1)], sem)


def _dispatch_kernel(pos_ref, x_ref, sh_ref, sc_ref, init_ref, out_ref, hbuf, sem):
    del init_ref
    i = pl.program_id(0)
    hbuf[...] = x_ref[...] * (1.0 + sc_ref[0]) + sh_ref[0]

    def start(r, carry):
        for k in range(2):
            _row_copy(hbuf, r, out_ref, pos_ref[2 * (i * TM + r) + k], sem).start()
        return carry
    lax.fori_loop(0, TM, start, 0)

    def wait(r, carry):
        for k in range(2):
            _row_copy(hbuf, 0, out_ref, 0, sem).wait()
        return carry
    lax.fori_loop(0, TM, wait, 0)


def _dispatch(pos, x, shift, scale, n_sorted, *, kind0):
    nt = x.shape[0] // TM
    init = jnp.zeros((n_sorted, D_MODEL), F32)
    return pl.pallas_call(
        _dispatch_kernel,
        grid_spec=pltpu.PrefetchScalarGridSpec(
            num_scalar_prefetch=1,
            grid=(nt,),
            in_specs=[_row_spec(D_MODEL), _mod_spec(kind0), _mod_spec(kind0),
                      pl.BlockSpec(memory_space=pl.ANY)],
            out_specs=pl.BlockSpec(memory_space=pl.ANY),
            scratch_shapes=[pltpu.VMEM((TM, D_MODEL), F32), pltpu.SemaphoreType.DMA],
        ),
        out_shape=jax.ShapeDtypeStruct((n_sorted, D_MODEL), F32),
        input_output_aliases={4: 0},
        compiler_params=_params("arbitrary"),
        name="moe_dispatch",
    )(pos, x, shift, scale, init)


def _ffn_kernel(te_ref, nv_ref, a_ref, wg_ref, wu_ref, wd_ref, o_ref):
    i = pl.program_id(0)

    @pl.when(i < nv_ref[0])
    def _():
        a = a_ref[...].astype(BF16)
        g = _dot(a, wg_ref[0].astype(BF16))
        u = _dot(a, wu_ref[0].astype(BF16))
        act = (g * jax.nn.sigmoid(g) * u).astype(BF16)
        o_ref[...] = _dot(act, wd_ref[0].astype(BF16))

    @pl.when(i >= nv_ref[0])
    def _():
        o_ref[...] = jnp.zeros(o_ref.shape, F32)


def _expert_ffn(tile_expert, n_valid, a_sorted, w_gate, w_up, w_down):
    n_tiles = a_sorted.shape[0] // TM
    return pl.pallas_call(
        _ffn_kernel,
        grid_spec=pltpu.PrefetchScalarGridSpec(
            num_scalar_prefetch=2,
            grid=(n_tiles,),
            in_specs=[pl.BlockSpec((TM, D_MODEL), lambda i, te, nv: (i, 0)),
                      pl.BlockSpec((1, D_MODEL, EXPERT_FF), lambda i, te, nv: (te[i], 0, 0)),
                      pl.BlockSpec((1, D_MODEL, EXPERT_FF), lambda i, te, nv: (te[i], 0, 0)),
                      pl.BlockSpec((1, EXPERT_FF, D_MODEL), lambda i, te, nv: (te[i], 0, 0))],
            out_specs=pl.BlockSpec((TM, D_MODEL), lambda i, te, nv: (i, 0)),
        ),
        out_shape=jax.ShapeDtypeStruct(a_sorted.shape, F32),
        compiler_params=_params("arbitrary"),
        name="moe_experts",
    )(tile_expert, n_valid, a_sorted, w_gate, w_up, w_down)


def _combine_kernel(pos_ref, y_ref, x_ref, meta_ref, gate_ref, g_ref, b_ref, *rest, emit_h):
    if emit_h:
        nsh_ref, nsc_ref, o_ref, h_ref, buf0, buf1, sem = rest
    else:
        o_ref, buf0, buf1, sem = rest
    i = pl.program_id(0)

    def start(r, carry):
        for k, buf in enumerate((buf0, buf1)):
            _row_copy(y_ref, pos_ref[2 * (i * TM + r) + k], buf, r, sem).start()
        return carry
    lax.fori_loop(0, TM, start, 0)

    def wait(r, carry):
        for buf in (buf0, buf1):
            _row_copy(y_ref, 0, buf, 0, sem).wait()
        return carry
    lax.fori_loop(0, TM, wait, 0)

    meta = meta_ref[...]
    moe = meta[:, 4:5] * buf0[...] + meta[:, 5:6] * buf1[...]
    y = ALPHA * x_ref[...] + gate_ref[0] * moe
    out = _layer_norm(y, g_ref[...], b_ref[...])
    o_ref[...] = out
    if emit_h:
        h_ref[...] = (out * (1.0 + nsc_ref[0]) + nsh_ref[0]).astype(h_ref.dtype)


def _combine(pos, y_sorted, x, meta, gate, ln_g, ln_b, next_mod, *, kind0):
    nt = x.shape[0] // TM
    emit_h = next_mod is not None
    mspec = _mod_spec(kind0)
    vec = _full_spec((1, D_MODEL))
    ospec = _row_spec(D_MODEL)
    in_specs = [pl.BlockSpec(memory_space=pl.ANY), ospec, _row_spec(LANES), mspec, vec, vec]
    args = [pos, y_sorted, x, meta, gate, ln_g, ln_b]
    out_specs = [ospec]
    out_shape = [jax.ShapeDtypeStruct((nt * TM, D_MODEL), F32)]
    if emit_h:
        in_specs += [mspec, mspec]
        args += list(next_mod)
        out_specs.append(ospec)
        out_shape.append(jax.ShapeDtypeStruct((nt * TM, D_MODEL), BF16))
    res = pl.pallas_call(
        functools.partial(_combine_kernel, emit_h=emit_h),
        grid_spec=pltpu.PrefetchScalarGridSpec(
            num_scalar_prefetch=1,
            grid=(nt,),
            in_specs=in_specs,
            out_specs=out_specs,
            scratch_shapes=[pltpu.VMEM((TM, D_MODEL), F32), pltpu.VMEM((TM, D_MODEL), F32),
                            pltpu.SemaphoreType.DMA],
        ),
        out_shape=out_shape,
        compiler_params=_params("arbitrary"),
        name="moe_combine",
    )(*args)
    return res if emit_h else (res[0], None)


def _moe_layer(x, mods, w_r, b_r, w_gate, w_up, w_down, ln_g, ln_b, next_mod, *, kind0, expert0):
    shift, scale, gate = mods
    n_tok = x.shape[0]
    n_tiles = -(-(2 * n_tok + N_EXPERTS * (TM - 1)) // TM)
    meta, cnt = _router(x, shift, scale, w_r, b_r, kind0=kind0)

    counts = cnt[0, :N_EXPERTS].astype(jnp.int32)
    padded = ((counts + TM - 1) // TM) * TM
    ends = jnp.cumsum(padded)
    offs = ends - padded
    experts = meta[:, 0:2].astype(jnp.int32)
    ranks = meta[:, 2:4].astype(jnp.int32)
    pos = (offs[experts] + ranks).reshape(-1)
    n_valid = ends[-1] // TM
    tile_start = jnp.arange(n_tiles, dtype=jnp.int32) * TM
    te = jnp.sum(tile_start[:, None] >= ends[None, :], axis=1).astype(jnp.int32)
    te_last = jnp.sum((n_valid - 1) * TM >= ends).astype(jnp.int32)
    te = jnp.where(tile_start < ends[-1], te, te_last) + expert0

    a_sorted = _dispatch(pos, x, shift, scale, n_tiles * TM, kind0=kind0)
    y_sorted = _expert_ffn(te, n_valid.reshape(1).astype(jnp.int32), a_sorted, w_gate, w_up, w_down)
    return _combine(pos, y_sorted, x, meta, gate, ln_g, ln_b, next_mod, kind0=kind0)


def _rope_tables():
    rows = SEQ // GRID_W
    row = jnp.repeat(jnp.arange(rows, dtype=F32), GRID_W)
    col = jnp.tile(jnp.arange(GRID_W, dtype=F32), rows)
    axis_dim = HEAD_DIM // 2
    inv = ROPE_THETA ** (-jnp.arange(0, axis_dim, 2, dtype=F32) / axis_dim)
    cr, sr = jnp.cos(row[:, None] * inv), jnp.sin(row[:, None] * inv)
    cc, sc = jnp.cos(col[:, None] * inv), jnp.sin(col[:, None] * inv)
    z = jnp.zeros_like(sr)
    c_tab = jnp.concatenate([cr, cr, cc, cc], axis=1)
    a_tab = jnp.concatenate([-sr, z, -sc, z], axis=1)
    b_tab = jnp.concatenate([z, sr, z, sc], axis=1)
    ctx = (jnp.ones((CTX_LEN, HEAD_DIM), F32), jnp.zeros((CTX_LEN, HEAD_DIM), F32), jnp.zeros((CTX_LEN, HEAD_DIM), F32))
    return tuple(jnp.concatenate([c0, t], axis=0) for c0, t in zip(ctx, (c_tab, a_tab, b_tab)))


def kernel(x, c, ctx, c_ctx, mod_w, mod_b, ln_g, ln_b, rt_grp_w, rt_grp_b, rt_exp_w, rt_exp_b,
           ex_w_gate, ex_w_up, ex_w_down, conv_in_w, conv_w, conv_out_w, pool_w, pool_scale,
           gqa_qkv_w, gqa_qk_norm, gqa_out_w, diff_qkv_w, diff_lambda, diff_subln, diff_out_w):
    assert x.shape == (1, SEQ, D_MODEL) and ctx.shape == (1, CTX_LEN, D_MODEL)
    xs = jnp.concatenate([ctx[0], x[0]], axis=0)

    c_cols = jnp.zeros((D_MODEL, LANES), F32).at[:, 0].set(c_ctx).at[:, 1].set(c[0])
    mods = _modulation(c_cols, mod_w, mod_b)
    mods = mods[:, :2].reshape(DEPTH, 2, N_MOD, 1, D_MODEL)
    mod = lambda layer, m: mods[layer, :, m]

    w_r = jnp.zeros((DEPTH, D_MODEL, LANES), F32)
    w_r = w_r.at[:, :, :N_GROUPS].set(rt_grp_w).at[:, :, N_GROUPS:N_GROUPS + N_EXPERTS].set(rt_exp_w)
    b_r = jnp.zeros((DEPTH, 1, LANES), F32)
    b_r = b_r.at[:, 0, :N_GROUPS].set(rt_grp_b).at[:, 0, N_GROUPS:N_GROUPS + N_EXPERTS].set(rt_exp_b)
    rope = _rope_tables()
    sm_scale = HEAD_DIM ** -0.5 * math.log2(math.e)
    bf = lambda w: w.astype(BF16)

    stack = lambda w: w.reshape((DEPTH * N_EXPERTS,) + w.shape[2:])
    w_gate, w_up, w_down = stack(ex_w_gate), stack(ex_w_up), stack(ex_w_down)

    def moe(layer, x_in, next_mod, kind0=0):
        return _moe_layer(x_in, (mod(layer, 3), mod(layer, 4), mod(layer, 5)), w_r[layer], b_r[layer],
                          w_gate, w_up, w_down, ln_g[layer, 1:2], ln_b[layer, 1:2], next_mod,
                          kind0=kind0, expert0=layer * N_EXPERTS)

    h = _modulate(xs, mod(0, 0), mod(0, 1))
    gb, p = _conv_in(h, bf(conv_in_w[0]))
    r = _conv_mix(p, gb, conv_w[0])
    xs = _mixer_out(r, bf(conv_out_w[0]), xs, mod(0, 2), ln_g[0, 0:1], ln_b[0, 0:1], kind0=0)
    xs, _ = moe(0, xs, None)

    xs = _pool_layer(xs, mod(1, 0), mod(1, 1), mod(1, 2), bf(pool_w[0]), pool_scale[0:1],
                     ln_g[1, 0:1], ln_b[1, 0:1])
    xs, h = moe(1, xs, (mod(2, 0), mod(2, 1)))

    wqkv = bf(gqa_qkv_w[0])
    nq, nkv = GQA_HEADS * HEAD_DIM, GQA_KV_HEADS * HEAD_DIM
    qt = _project(h, wqkv, 0, nq, gain=gqa_qk_norm[0, 0:1], rope_tabs=rope, scale=sm_scale)
    k = _project(h, wqkv, nq, nkv, gain=gqa_qk_norm[0, 1:2], rope_tabs=rope)
    vt = _project(h, wqkv, nq + nkv, nkv, chunk_width=HEAD_DIM)
    o = _gqa_attention(qt, k, vt)
    xs = _mixer_out(o, bf(gqa_out_w[0]), xs, mod(2, 2), ln_g[2, 0:1], ln_b[2, 0:1], kind0=0)
    xs, h = moe(2, xs, (mod(3, 0), mod(3, 1)))

    lam_init = 0.8 - 0.6 * math.exp(-0.3 * 3)
    wqkv = bf(diff_qkv_w[0])
    qt = _project(h, wqkv, 0, D_MODEL, rope_tabs=rope, scale=sm_scale, tile0=1)
    k = _project(h, wqkv, D_MODEL, D_MODEL, rope_tabs=rope)
    vt = _project(h, wqkv, 2 * D_MODEL, D_MODEL, chunk_width=2 * HEAD_DIM)
    o = _diff_attention(qt, k, vt, diff_lambda[0], diff_subln[0:1], lam_init)
    xs = _mixer_out(o, bf(diff_out_w[0]), xs, mod(3, 2), ln_g[3, 0:1], ln_b[3, 0:1], kind0=1)
    out, _ = moe(3, xs, None, kind0=1)
    return out[None]
```

```python
import functools
import math

import jax
import jax.numpy as jnp
from jax import lax
from jax.experimental import pallas as pl
from jax.experimental.pallas import tpu as pltpu

D_MODEL = 2048
SEQ = 8192
DEPTH = 4
GRID_W = 64
CTX_LEN = 256
ALPHA = (2.0 * DEPTH) ** 0.25
ROPE_THETA = 10000.0
LN_EPS = 1e-6
RMS_EPS = 1e-6
N_MOD = 6
HEAD_DIM = 128
GQA_HEADS = D_MODEL // HEAD_DIM
GQA_KV_HEADS = GQA_HEADS // 4
GQA_GROUP = GQA_HEADS // GQA_KV_HEADS
DIFF_HEADS = D_MODEL // (2 * HEAD_DIM)
POOL_WINDOWS = (2, 4, 8, 16)
POOL_GROUP = D_MODEL // len(POOL_WINDOWS)
N_GROUPS = 4
EXPERTS_PER_GROUP = 4
N_EXPERTS = N_GROUPS * EXPERTS_PER_GROUP
EXPERT_FF = 3 * D_MODEL // 8

LANES = 128
SUBLANES = 8
TM = 256
T_ALL = CTX_LEN + SEQ
NT_ALL = T_ALL // TM
HALO = SUBLANES
VMEM_LIMIT = 56 * 1024 * 1024

assert CTX_LEN == TM and SEQ % TM == 0 and DEPTH == 4

F32 = jnp.float32
BF16 = jnp.bfloat16


def _params(*sem):
    return pltpu.CompilerParams(dimension_semantics=sem, vmem_limit_bytes=VMEM_LIMIT)


def _dot(a, b):
    return jnp.dot(a, b, preferred_element_type=F32)


def _layer_norm(y, g, b):
    mu = jnp.mean(y, axis=-1, keepdims=True)
    d = y - mu
    var = jnp.mean(d * d, axis=-1, keepdims=True)
    return d * lax.rsqrt(var + LN_EPS) * g + b


def _mod_spec(kind0):
    return pl.BlockSpec((1, 1, D_MODEL), lambda i, *_: (jnp.minimum(i + kind0, 1), 0, 0))


def _row_spec(width, tile0=0):
    return pl.BlockSpec((TM, width), lambda i, *_: (i + tile0, 0))


def _full_spec(shape):
    nd = len(shape)
    return pl.BlockSpec(shape, lambda i, *_: (0,) * nd)


MOD_TN = 512


def _mod_kernel(c_ref, w_ref, b_ref, o_ref):
    w = w_ref[0]
    rows = []
    for r in range(2):
        c = c_ref[:, r:r + 1]
        s = c * jax.nn.sigmoid(c)
        rows.append(jnp.sum(w * s, axis=0, keepdims=True))
    rows.append(jnp.zeros((SUBLANES - 2, MOD_TN), F32))
    o_ref[0] = jnp.concatenate(rows, axis=0) + b_ref[0]


def _modulation(c_cols, mod_w, mod_b):
    n_out = N_MOD * D_MODEL
    return pl.pallas_call(
        _mod_kernel,
        grid=(DEPTH, n_out // MOD_TN),
        in_specs=[
            pl.BlockSpec((D_MODEL, LANES), lambda l, j: (0, 0)),
            pl.BlockSpec((1, D_MODEL, MOD_TN), lambda l, j: (l, 0, j)),
            pl.BlockSpec((1, 1, MOD_TN), lambda l, j: (l, 0, j)),
        ],
        out_specs=pl.BlockSpec((1, SUBLANES, MOD_TN), lambda l, j: (l, 0, j)),
        out_shape=jax.ShapeDtypeStruct((DEPTH, SUBLANES, n_out), F32),
        compiler_params=_params("arbitrary", "arbitrary"),
        name="modulation",
    )(c_cols, mod_w, mod_b.reshape(DEPTH, 1, n_out))


def _modulate_kernel(x_ref, sh_ref, sc_ref, o_ref):
    o_ref[...] = (x_ref[...] * (1.0 + sc_ref[0]) + sh_ref[0]).astype(o_ref.dtype)


def _modulate(x, shift, scale):
    return pl.pallas_call(
        _modulate_kernel,
        grid=(NT_ALL,),
        in_specs=[_row_spec(D_MODEL), _mod_spec(0), _mod_spec(0)],
        out_specs=_row_spec(D_MODEL),
        out_shape=jax.ShapeDtypeStruct((T_ALL, D_MODEL), BF16),
        compiler_params=_params("arbitrary"),
        name="modulate",
    )(x, shift, scale)


CONV_TN = 512


def _conv_in_kernel(h_ref, wb_ref, wc_ref, wv_ref, gb_ref, p_ref):
    h = h_ref[...]
    gb_ref[...] = _dot(h, wb_ref[...])
    p_ref[...] = _dot(h, wc_ref[...]) * _dot(h, wv_ref[...])


def _conv_in(h, w_in):
    nb = D_MODEL // CONV_TN
    wspec = lambda off: pl.BlockSpec((D_MODEL, CONV_TN), lambda j, i: (0, j + off * nb))
    ospec = pl.BlockSpec((TM, CONV_TN), lambda j, i: (i, j))
    return pl.pallas_call(
        _conv_in_kernel,
        grid=(nb, NT_ALL),
        in_specs=[pl.BlockSpec((TM, D_MODEL), lambda j, i: (i, 0)), wspec(0), wspec(1), wspec(2)],
        out_specs=[ospec, ospec],
        out_shape=[jax.ShapeDtypeStruct((T_ALL, D_MODEL), F32)] * 2,
        compiler_params=_params("arbitrary", "arbitrary"),
        name="conv_in",
    )(h, w_in, w_in, w_in)


def _halo_flags(i, nt):
    has_prev = (i >= 2).astype(F32)
    has_next = jnp.logical_and(i >= 1, i < nt - 1).astype(F32)
    return has_prev, has_next


def _conv_mix_kernel(p_ref, pp_ref, pn_ref, gb_ref, cw_ref, r_ref, ext):
    has_prev, has_next = _halo_flags(pl.program_id(0), NT_ALL)
    ext[0:HALO] = pp_ref[...] * has_prev
    ext[HALO:HALO + TM] = p_ref[...]
    ext[HALO + TM:] = pn_ref[...] * has_next
    q = (ext[HALO - 1:HALO - 1 + TM] * cw_ref[0:1]
         + ext[HALO:HALO + TM] * cw_ref[1:2]
         + ext[HALO + 1:HALO + 1 + TM] * cw_ref[2:3])
    r_ref[...] = (gb_ref[...] * q).astype(r_ref.dtype)


def _halo_specs(width):
    per = TM // HALO
    last = T_ALL // HALO - 1
    prev = pl.BlockSpec((HALO, width), lambda i: (jnp.maximum(i * per - 1, 0), 0))
    nxt = pl.BlockSpec((HALO, width), lambda i: (jnp.minimum((i + 1) * per, last), 0))
    return prev, nxt


def _conv_mix(p, gb, conv_w):
    prev, nxt = _halo_specs(D_MODEL)
    return pl.pallas_call(
        _conv_mix_kernel,
        grid=(NT_ALL,),
        in_specs=[_row_spec(D_MODEL), prev, nxt, _row_spec(D_MODEL), _full_spec((3, D_MODEL))],
        out_specs=_row_spec(D_MODEL),
        out_shape=jax.ShapeDtypeStruct((T_ALL, D_MODEL), BF16),
        scratch_shapes=[pltpu.VMEM((TM + 2 * HALO, D_MODEL), F32)],
        compiler_params=_params("arbitrary"),
        name="conv_mix",
    )(p, p, p, gb, conv_w)


def _mixer_out_kernel(r_ref, w_ref, x_ref, gate_ref, g_ref, b_ref, o_ref):
    o = _dot(r_ref[...], w_ref[...])
    y = ALPHA * x_ref[...] + gate_ref[0] * o
    o_ref[...] = _layer_norm(y, g_ref[...], b_ref[...])


def _mixer_out(r, w_out, x, gate, ln_g, ln_b, *, kind0):
    nt = r.shape[0] // TM
    return pl.pallas_call(
        _mixer_out_kernel,
        grid=(nt,),
        in_specs=[_row_spec(D_MODEL), _full_spec((D_MODEL, D_MODEL)), _row_spec(D_MODEL, NT_ALL - nt),
                  _mod_spec(kind0), _full_spec((1, D_MODEL)), _full_spec((1, D_MODEL))],
        out_specs=_row_spec(D_MODEL),
        out_shape=jax.ShapeDtypeStruct((nt * TM, D_MODEL), F32),
        compiler_params=_params("arbitrary"),
        name="mixer_out",
    )(r, w_out, x, gate, ln_g, ln_b)


def _pool_kernel(x_ref, xp_ref, xn_ref, sh_ref, sc_ref, gate_ref, w_ref, ps_ref, g_ref, b_ref, o_ref, ext):
    i = pl.program_id(0)
    has_prev, has_next = _halo_flags(i, NT_ALL)
    one_sc = 1.0 + sc_ref[0]
    sh = sh_ref[0]
    x = x_ref[...]
    h = x * one_sc + sh
    ext[0:HALO] = (xp_ref[...] * one_sc + sh) * has_prev
    ext[HALO:HALO + TM] = h
    ext[HALO + TM:] = (xn_ref[...] * one_sc + sh) * has_next
    t_loc = lax.broadcasted_iota(jnp.int32, (TM, 1), 0) + jnp.maximum(i - 1, 0) * TM
    n_seq = jnp.where(i == 0, CTX_LEN, SEQ)
    mixed = []
    for g, win in enumerate(POOL_WINDOWS):
        c0 = g * POOL_GROUP
        half = win // 2
        s = ext[HALO - half:HALO - half + TM, c0:c0 + POOL_GROUP]
        for j in range(1 - half, win - half):
            s = s + ext[HALO + j:HALO + j + TM, c0:c0 + POOL_GROUP]
        lo = jnp.maximum(t_loc - half, 0)
        hi = jnp.minimum(t_loc + (win - half), n_seq)
        cnt = (hi - lo).astype(F32)
        pooled = s / cnt - h[:, c0:c0 + POOL_GROUP]
        mixed.append(_dot(pooled.astype(BF16), w_ref[g]))
    o = jnp.concatenate(mixed, axis=1) * ps_ref[...]
    y = ALPHA * x + gate_ref[0] * o
    o_ref[...] = _layer_norm(y, g_ref[...], b_ref[...])


def _pool_layer(x, shift, scale, gate, w_grp, pool_scale, ln_g, ln_b):
    prev, nxt = _halo_specs(D_MODEL)
    return pl.pallas_call(
        _pool_kernel,
        grid=(NT_ALL,),
        in_specs=[_row_spec(D_MODEL), prev, nxt, _mod_spec(0), _mod_spec(0), _mod_spec(0),
                  _full_spec((len(POOL_WINDOWS), POOL_GROUP, POOL_GROUP)), _full_spec((1, D_MODEL)),
                  _full_spec((1, D_MODEL)), _full_spec((1, D_MODEL))],
        out_specs=_row_spec(D_MODEL),
        out_shape=jax.ShapeDtypeStruct((T_ALL, D_MODEL), F32),
        scratch_shapes=[pltpu.VMEM((TM + 2 * HALO, D_MODEL), F32)],
        compiler_params=_params("arbitrary"),
        name="pool_layer",
    )(x, x, x, shift, scale, gate, w_grp, pool_scale, ln_g, ln_b)


PROJ_TN = 512


def _proj_kernel(*refs, norm, rope, scale, chunk_width, group_heads):
    refs = list(refs)
    h_ref, w_ref = refs[:2]
    rest = refs[2:]
    gain_ref = rest.pop(0) if norm else None
    if rope:
        c_ref, a_ref, b_ref = rest[:3]
        rest = rest[3:]
    (o_ref,) = rest
    acc = _dot(h_ref[...], w_ref[...])
    if chunk_width:
        for hh in range(PROJ_TN // chunk_width):
            o_ref[hh, 0] = acc[:, hh * chunk_width:(hh + 1) * chunk_width].T.astype(o_ref.dtype)
        return
    for hh in range(PROJ_TN // HEAD_DIM):
        xh = acc[:, hh * HEAD_DIM:(hh + 1) * HEAD_DIM]
        if norm:
            ms = jnp.mean(xh * xh, axis=-1, keepdims=True)
            xh = xh * lax.rsqrt(ms + RMS_EPS) * gain_ref[...]
        if rope:
            xh = (xh * c_ref[...] + pltpu.roll(xh, HEAD_DIM - 32, axis=1) * a_ref[...]
                  + pltpu.roll(xh, 32, axis=1) * b_ref[...])
        if scale != 1.0:
            xt = (xh * scale).T.astype(o_ref.dtype)
            if group_heads:
                o_ref[0, 0, :, hh * TM:(hh + 1) * TM] = xt
            else:
                o_ref[hh * HEAD_DIM:(hh + 1) * HEAD_DIM, :] = xt
        else:
            o_ref[:, hh * HEAD_DIM:(hh + 1) * HEAD_DIM] = xh.astype(o_ref.dtype)


def _project(h, w, col0, width, *, gain=None, rope_tabs=None, scale=1.0, tile0=0, chunk_width=0,
             group_heads=False):
    nt = NT_ALL - tile0
    nb = width // PROJ_TN
    off = col0 // PROJ_TN
    in_specs = [pl.BlockSpec((TM, D_MODEL), lambda j, i: (i + tile0, 0)),
                pl.BlockSpec((D_MODEL, PROJ_TN), lambda j, i: (0, j + off))]
    args = [h, w]
    if gain is not None:
        in_specs.append(pl.BlockSpec((1, HEAD_DIM), lambda j, i: (0, 0)))
        args.append(gain)
    if rope_tabs is not None:
        in_specs += [pl.BlockSpec((TM, HEAD_DIM), lambda j, i: (i + tile0, 0))] * 3
        args += list(rope_tabs)
    if chunk_width:
        per = PROJ_TN // chunk_width
        out_spec = pl.BlockSpec((per, 1, chunk_width, TM), lambda j, i: (j, i, 0, 0))
        out_shape = jax.ShapeDtypeStruct((width // chunk_width, nt, chunk_width, TM), BF16)
    elif scale != 1.0 and group_heads:
        per = PROJ_TN // HEAD_DIM
        out_spec = pl.BlockSpec((1, 1, HEAD_DIM, per * TM), lambda j, i: (j, i, 0, 0))
        out_shape = jax.ShapeDtypeStruct((nb, nt, HEAD_DIM, per * TM), BF16)
    elif scale != 1.0:
        out_spec = pl.BlockSpec((PROJ_TN, TM), lambda j, i: (j, i))
        out_shape = jax.ShapeDtypeStruct((width, nt * TM), BF16)
    else:
        out_spec = pl.BlockSpec((TM, PROJ_TN), lambda j, i: (i, j))
        out_shape = jax.ShapeDtypeStruct((nt * TM, width), BF16)
    kern = functools.partial(_proj_kernel, norm=gain is not None, rope=rope_tabs is not None, scale=scale,
                             chunk_width=chunk_width, group_heads=group_heads)
    return pl.pallas_call(
        kern,
        grid=(nb, nt),
        in_specs=in_specs,
        out_specs=out_spec,
        out_shape=out_shape,
        compiler_params=_params("arbitrary", "arbitrary"),
        name="project",
    )(*args)


def _softmax_update(s, vt_c, m_ref, l_ref, acc_ref):
    m_prev = m_ref[...]
    m_new = jnp.maximum(m_prev, jnp.max(s, axis=0, keepdims=True))
    p = jnp.exp2(s - m_new)
    alpha = jnp.exp2(m_prev - m_new)
    l_ref[...] = alpha * l_ref[...] + jnp.sum(p, axis=0, keepdims=True)
    acc_ref[...] = alpha * acc_ref[...] + _dot(vt_c, p.astype(BF16))
    m_ref[...] = m_new


def _key_tile(k_ref, j):
    return k_ref[j * TM:(j + 1) * TM, :]


def _softmax_pipeline(steps, m_ref, l_ref, acc_ref):
    s_next = steps[0][0]()
    for i, (_, vt_c, stream) in enumerate(steps):
        s_cur = s_next
        if i + 1 < len(steps):
            s_next = steps[i + 1][0]()
        _softmax_update(s_cur, vt_c, m_ref.at[stream], l_ref.at[stream], acc_ref.at[stream])


def _attend(make_steps, context_queries, m_ref, l_ref, acc_ref):
    m_ref[...] = jnp.full(m_ref.shape, -jnp.inf, F32)
    l_ref[...] = jnp.zeros(l_ref.shape, F32)
    acc_ref[...] = jnp.zeros(acc_ref.shape, F32)

    if context_queries is not None:
        @pl.when(context_queries)
        def _():
            _softmax_pipeline(make_steps(0), m_ref, l_ref, acc_ref)

    def all_tiles():
        steps = []
        for j in range(NT_ALL):
            steps += make_steps(j)
        _softmax_pipeline(steps, m_ref, l_ref, acc_ref)

    if context_queries is None:
        all_tiles()
    else:
        pl.when(jnp.logical_not(context_queries))(all_tiles)


def _gqa_kernel(qt_ref, k_ref, vt_ref, o_ref, m_ref, l_ref, acc_ref):
    wide = GQA_STREAM_HEADS * TM

    def make_steps(j):
        k_c = _key_tile(k_ref, j)
        vt_c = vt_ref[0, j]
        return [(functools.partial(lambda s: _dot(k_c, qt_ref[0, 0, :, s * wide:(s + 1) * wide]), s), vt_c, s)
                for s in range(GQA_GROUP // GQA_STREAM_HEADS)]
    _attend(make_steps, pl.program_id(1) == 0, m_ref, l_ref, acc_ref)

    for g in range(GQA_GROUP):
        s, c = divmod(g, GQA_STREAM_HEADS)
        o = acc_ref[s, :, c * TM:(c + 1) * TM] / l_ref[s, :, c * TM:(c + 1) * TM]
        o_ref[:, g * HEAD_DIM:(g + 1) * HEAD_DIM] = o.T.astype(o_ref.dtype)


GQA_STREAM_HEADS = 2


def _gqa_attention(qt, k, vt):
    wide = GQA_GROUP * TM
    streams = GQA_GROUP // GQA_STREAM_HEADS
    sw = GQA_STREAM_HEADS * TM
    return pl.pallas_call(
        _gqa_kernel,
        grid=(GQA_KV_HEADS, NT_ALL),
        in_specs=[pl.BlockSpec((1, 1, HEAD_DIM, wide), lambda kv, i: (kv, i, 0, 0)),
                  pl.BlockSpec((T_ALL, HEAD_DIM), lambda kv, i: (0, kv)),
                  pl.BlockSpec((1, NT_ALL, HEAD_DIM, TM), lambda kv, i: (kv, 0, 0, 0))],
        out_specs=pl.BlockSpec((TM, GQA_GROUP * HEAD_DIM), lambda kv, i: (i, kv)),
        out_shape=jax.ShapeDtypeStruct((T_ALL, D_MODEL), BF16),
        scratch_shapes=[pltpu.VMEM((streams, 1, sw), F32), pltpu.VMEM((streams, 1, sw), F32),
                        pltpu.VMEM((streams, HEAD_DIM, sw), F32)],
        compiler_params=_params("arbitrary", "arbitrary"),
        name="gqa_attention",
    )(qt, k, vt)


def _diff_kernel(qt_ref, k_ref, vt_ref, lam_ref, sub_ref, o_ref, m_ref, l_ref, acc_ref, *, lam_init):
    def make_steps(j):
        k_c = _key_tile(k_ref, j)
        vt_c = vt_ref[0, j]
        return [(functools.partial(lambda s: _dot(k_c[:, s * HEAD_DIM:(s + 1) * HEAD_DIM],
                                                  qt_ref[s * HEAD_DIM:(s + 1) * HEAD_DIM, :]), s), vt_c, s)
                for s in range(2)]
    _attend(make_steps, None, m_ref, l_ref, acc_ref)

    lv = lam_ref[...]
    lam = (jnp.exp(jnp.sum(lv[0:1] * lv[1:2], axis=-1, keepdims=True))
           - jnp.exp(jnp.sum(lv[2:3] * lv[3:4], axis=-1, keepdims=True)) + lam_init)
    o = (acc_ref[0] / l_ref[0] - lam * (acc_ref[1] / l_ref[1])).T
    ms = jnp.mean(o * o, axis=-1, keepdims=True)
    o = o * lax.rsqrt(ms + RMS_EPS) * sub_ref[...] * (1.0 - lam_init)
    o_ref[...] = o.astype(o_ref.dtype)


def _diff_attention(qt, k, vt, lam_vecs, subln, lam_init):
    wv = 2 * HEAD_DIM
    return pl.pallas_call(
        functools.partial(_diff_kernel, lam_init=lam_init),
        grid=(DIFF_HEADS, SEQ // TM),
        in_specs=[pl.BlockSpec((wv, TM), lambda hd, i: (hd, i)),
                  pl.BlockSpec((T_ALL, wv), lambda hd, i: (0, hd)),
                  pl.BlockSpec((1, NT_ALL, wv, TM), lambda hd, i: (hd, 0, 0, 0)),
                  pl.BlockSpec((4, HEAD_DIM), lambda hd, i: (0, 0)), pl.BlockSpec((1, wv), lambda hd, i: (0, 0))],
        out_specs=pl.BlockSpec((TM, wv), lambda hd, i: (i, hd)),
        out_shape=jax.ShapeDtypeStruct((SEQ, D_MODEL), BF16),
        scratch_shapes=[pltpu.VMEM((2, 1, TM), F32), pltpu.VMEM((2, 1, TM), F32), pltpu.VMEM((2, wv, TM), F32)],
        compiler_params=_params("arbitrary", "arbitrary"),
        name="diff_attention",
    )(qt, k, vt, lam_vecs, subln)


def _lane_pick(vals, lane, idx):
    return jnp.sum(jnp.where(lane == idx, vals, 0.0), axis=-1, keepdims=True)


def _router_kernel(x_ref, sh_ref, sc_ref, w_ref, b_ref, meta_ref, cnt_ref, carry):
    i = pl.program_id(0)

    @pl.when(i == 0)
    def _():
        carry[...] = jnp.zeros(carry.shape, F32)

    h = x_ref[...] * (1.0 + sc_ref[0]) + sh_ref[0]
    logits = jnp.dot(h, w_ref[...], preferred_element_type=F32, precision=lax.Precision.HIGHEST) + b_ref[...]
    lane = lax.broadcasted_iota(jnp.int32, (TM, LANES), 1).astype(F32)
    neg = -jnp.inf
    big = float(LANES)

    grp = jnp.where(lane < N_GROUPS, logits, neg)
    gmax = jnp.max(grp, axis=-1, keepdims=True)
    gstar = jnp.min(jnp.where(grp == gmax, lane, big), axis=-1, keepdims=True)
    p_group = 1.0 / jnp.sum(jnp.exp(grp - gmax), axis=-1, keepdims=True)

    e_lo = N_GROUPS + EXPERTS_PER_GROUP * gstar
    own = jnp.logical_and(lane >= e_lo, lane < e_lo + EXPERTS_PER_GROUP)
    le = jnp.where(own, logits, neg)
    v1 = jnp.max(le, axis=-1, keepdims=True)
    i1 = jnp.min(jnp.where(le == v1, lane, big), axis=-1, keepdims=True)
    le2 = jnp.where(lane == i1, neg, le)
    v2 = jnp.max(le2, axis=-1, keepdims=True)
    i2 = jnp.min(jnp.where(le2 == v2, lane, big), axis=-1, keepdims=True)
    t2 = jnp.exp(v2 - v1)
    w1 = p_group / (1.0 + t2)
    w2 = p_group * t2 / (1.0 + t2)
    e1 = i1 - N_GROUPS
    e2 = i2 - N_GROUPS

    chosen = jnp.logical_or(lane == e1, lane == e2)
    onehot = jnp.where(chosen, 1.0, 0.0)
    r_io = lax.broadcasted_iota(jnp.int32, (TM, TM), 0)
    c_io = lax.broadcasted_iota(jnp.int32, (TM, TM), 1)
    tri = jnp.where(c_io < r_io, 1.0, 0.0).astype(BF16)
    before = carry[...] + _dot(tri, onehot.astype(BF16))
    r1 = _lane_pick(before, lane, e1)
    r2 = _lane_pick(before, lane, e2)
    carry[...] = carry[...] + jnp.sum(onehot, axis=0, keepdims=True)

    meta = jnp.where(lane == 0, e1, 0.0)
    for col, val in ((1, e2), (2, r1), (3, r2), (4, w1), (5, w2)):
        meta = jnp.where(lane == col, val, meta)
    meta_ref[...] = meta
    cnt_ref[...] = jnp.broadcast_to(carry[...], cnt_ref.shape)


def _router(x, shift, scale, w_r, b_r, *, kind0):
    nt = x.shape[0] // TM
    return pl.pallas_call(
        _router_kernel,
        grid=(nt,),
        in_specs=[_row_spec(D_MODEL), _mod_spec(kind0), _mod_spec(kind0),
                  _full_spec((D_MODEL, LANES)), _full_spec((1, LANES))],
        out_specs=[_row_spec(LANES), _full_spec((SUBLANES, LANES))],
        out_shape=[jax.ShapeDtypeStruct((nt * TM, LANES), F32), jax.ShapeDtypeStruct((SUBLANES, LANES), F32)],
        scratch_shapes=[pltpu.VMEM((1, LANES), F32)],
        compiler_params=_params("arbitrary"),
        name="router",
    )(x, shift, scale, w_r, b_r)


def _row_copy(src, src_row, dst, dst_row, sem):
    return pltpu.make_async_copy(src.at[pl.ds(src_row, 1)], dst.at[pl.ds(dst_row, 1)], sem)


def _dispatch_kernel(pos_ref, x_ref, sh_ref, sc_ref, init_ref, out_ref, hbuf, sem):
    del init_ref
    i = pl.program_id(0)
    hbuf[...] = x_ref[...] * (1.0 + sc_ref[0]) + sh_ref[0]

    def start(r, carry):
        for k in range(2):
            _row_copy(hbuf, r, out_ref, pos_ref[2 * (i * TM + r) + k], sem).start()
        return carry
    lax.fori_loop(0, TM, start, 0, unroll=ROW_DMA_UNROLL)
    for k in range(2):
        _tile_wait(hbuf, out_ref, sem)


ROW_DMA_UNROLL = 8


def _tile_wait(vmem_tile, hbm, sem):
    pltpu.make_async_copy(hbm.at[pl.ds(0, TM)], vmem_tile, sem).wait()


def _dispatch(pos, x, shift, scale, n_sorted, *, kind0):
    nt = x.shape[0] // TM
    init = jnp.zeros((n_sorted, D_MODEL), F32)
    return pl.pallas_call(
        _dispatch_kernel,
        grid_spec=pltpu.PrefetchScalarGridSpec(
            num_scalar_prefetch=1,
            grid=(nt,),
            in_specs=[_row_spec(D_MODEL), _mod_spec(kind0), _mod_spec(kind0),
                      pl.BlockSpec(memory_space=pl.ANY)],
            out_specs=pl.BlockSpec(memory_space=pl.ANY),
            scratch_shapes=[pltpu.VMEM((TM, D_MODEL), F32), pltpu.SemaphoreType.DMA],
        ),
        out_shape=jax.ShapeDtypeStruct((n_sorted, D_MODEL), F32),
        input_output_aliases={4: 0},
        compiler_params=_params("arbitrary"),
        name="moe_dispatch",
    )(pos, x, shift, scale, init)


def _ffn_kernel(te_ref, nv_ref, a_ref, wg_ref, wu_ref, wd_ref, o_ref):
    i = pl.program_id(0)

    @pl.when(i < nv_ref[0])
    def _():
        a = a_ref[...].astype(BF16)
        g = _dot(a, wg_ref[0].astype(BF16))
        u = _dot(a, wu_ref[0].astype(BF16))
        act = (g * jax.nn.sigmoid(g) * u).astype(BF16)
        o_ref[...] = _dot(act, wd_ref[0].astype(BF16))

    @pl.when(i >= nv_ref[0])
    def _():
        o_ref[...] = jnp.zeros(o_ref.shape, F32)


def _expert_ffn(tile_expert, n_valid, a_sorted, w_gate, w_up, w_down):
    n_tiles = a_sorted.shape[0] // TM
    return pl.pallas_call(
        _ffn_kernel,
        grid_spec=pltpu.PrefetchScalarGridSpec(
            num_scalar_prefetch=2,
            grid=(n_tiles,),
            in_specs=[pl.BlockSpec((TM, D_MODEL), lambda i, te, nv: (i, 0)),
                      pl.BlockSpec((1, D_MODEL, EXPERT_FF), lambda i, te, nv: (te[i], 0, 0)),
                      pl.BlockSpec((1, D_MODEL, EXPERT_FF), lambda i, te, nv: (te[i], 0, 0)),
                      pl.BlockSpec((1, EXPERT_FF, D_MODEL), lambda i, te, nv: (te[i], 0, 0))],
            out_specs=pl.BlockSpec((TM, D_MODEL), lambda i, te, nv: (i, 0)),
        ),
        out_shape=jax.ShapeDtypeStruct(a_sorted.shape, F32),
        compiler_params=_params("arbitrary"),
        name="moe_experts",
    )(tile_expert, n_valid, a_sorted, w_gate, w_up, w_down)


def _combine_kernel(pos_ref, y_ref, x_ref, meta_ref, gate_ref, g_ref, b_ref, *rest, emit_h, nt):
    if emit_h:
        nsh_ref, nsc_ref, o_ref, h_ref, bufs, sems = rest
    else:
        o_ref, bufs, sems = rest
    i = pl.program_id(0)

    def gather(tile, slot):
        def start(r, carry):
            for k in range(2):
                _row_copy(y_ref, pos_ref[2 * (tile * TM + r) + k], bufs.at[slot, k], r, sems.at[slot]).start()
            return carry
        lax.fori_loop(0, TM, start, 0, unroll=ROW_DMA_UNROLL)

    @pl.when(i == 0)
    def _():
        gather(0, 0)

    @pl.when(i + 1 < nt)
    def _():
        gather(i + 1, (i + 1) & 1)

    slot = i & 1
    for k in range(2):
        _tile_wait(bufs.at[slot, k], y_ref, sems.at[slot])

    meta = meta_ref[...]
    moe = meta[:, 4:5] * bufs[slot, 0] + meta[:, 5:6] * bufs[slot, 1]
    y = ALPHA * x_ref[...] + gate_ref[0] * moe
    out = _layer_norm(y, g_ref[...], b_ref[...])
    o_ref[...] = out
    if emit_h:
        h_ref[...] = (out * (1.0 + nsc_ref[0]) + nsh_ref[0]).astype(h_ref.dtype)


def _combine(pos, y_sorted, x, meta, gate, ln_g, ln_b, next_mod, *, kind0):
    nt = x.shape[0] // TM
    emit_h = next_mod is not None
    mspec = _mod_spec(kind0)
    vec = _full_spec((1, D_MODEL))
    ospec = _row_spec(D_MODEL)
    in_specs = [pl.BlockSpec(memory_space=pl.ANY), ospec, _row_spec(LANES), mspec, vec, vec]
    args = [pos, y_sorted, x, meta, gate, ln_g, ln_b]
    out_specs = [ospec]
    out_shape = [jax.ShapeDtypeStruct((nt * TM, D_MODEL), F32)]
    if emit_h:
        in_specs += [mspec, mspec]
        args += list(next_mod)
        out_specs.append(ospec)
        out_shape.append(jax.ShapeDtypeStruct((nt * TM, D_MODEL), BF16))
    res = pl.pallas_call(
        functools.partial(_combine_kernel, emit_h=emit_h, nt=nt),
        grid_spec=pltpu.PrefetchScalarGridSpec(
            num_scalar_prefetch=1,
            grid=(nt,),
            in_specs=in_specs,
            out_specs=out_specs,
            scratch_shapes=[pltpu.VMEM((2, 2, TM, D_MODEL), F32), pltpu.SemaphoreType.DMA((2,))],
        ),
        out_shape=out_shape,
        compiler_params=_params("arbitrary"),
        name="moe_combine",
    )(*args)
    return res if emit_h else (res[0], None)


def _moe_layer(x, mods, w_r, b_r, w_gate, w_up, w_down, ln_g, ln_b, next_mod, *, kind0, expert0):
    shift, scale, gate = mods
    n_tok = x.shape[0]
    n_tiles = -(-(2 * n_tok + N_EXPERTS * (TM - 1)) // TM)
    meta, cnt = _router(x, shift, scale, w_r, b_r, kind0=kind0)

    counts = cnt[0, :N_EXPERTS].astype(jnp.int32)
    padded = ((counts + TM - 1) // TM) * TM
    ends = jnp.cumsum(padded)
    offs = ends - padded
    experts = meta[:, 0:2].astype(jnp.int32)
    ranks = meta[:, 2:4].astype(jnp.int32)
    pos = (offs[experts] + ranks).reshape(-1)
    n_valid = ends[-1] // TM
    tile_start = jnp.arange(n_tiles, dtype=jnp.int32) * TM
    te = jnp.sum(tile_start[:, None] >= ends[None, :], axis=1).astype(jnp.int32)
    te_last = jnp.sum((n_valid - 1) * TM >= ends).astype(jnp.int32)
    te = jnp.where(tile_start < ends[-1], te, te_last) + expert0

    a_sorted = _dispatch(pos, x, shift, scale, n_tiles * TM, kind0=kind0)
    y_sorted = _expert_ffn(te, n_valid.reshape(1).astype(jnp.int32), a_sorted, w_gate, w_up, w_down)
    return _combine(pos, y_sorted, x, meta, gate, ln_g, ln_b, next_mod, kind0=kind0)


def _rope_tables():
    rows = SEQ // GRID_W
    row = jnp.repeat(jnp.arange(rows, dtype=F32), GRID_W)
    col = jnp.tile(jnp.arange(GRID_W, dtype=F32), rows)
    axis_dim = HEAD_DIM // 2
    inv = ROPE_THETA ** (-jnp.arange(0, axis_dim, 2, dtype=F32) / axis_dim)
    cr, sr = jnp.cos(row[:, None] * inv), jnp.sin(row[:, None] * inv)
    cc, sc = jnp.cos(col[:, None] * inv), jnp.sin(col[:, None] * inv)
    z = jnp.zeros_like(sr)
    c_tab = jnp.concatenate([cr, cr, cc, cc], axis=1)
    a_tab = jnp.concatenate([-sr, z, -sc, z], axis=1)
    b_tab = jnp.concatenate([z, sr, z, sc], axis=1)
    ctx = (jnp.ones((CTX_LEN, HEAD_DIM), F32), jnp.zeros((CTX_LEN, HEAD_DIM), F32), jnp.zeros((CTX_LEN, HEAD_DIM), F32))
    return tuple(jnp.concatenate([c0, t], axis=0) for c0, t in zip(ctx, (c_tab, a_tab, b_tab)))


def kernel(x, c, ctx, c_ctx, mod_w, mod_b, ln_g, ln_b, rt_grp_w, rt_grp_b, rt_exp_w, rt_exp_b,
           ex_w_gate, ex_w_up, ex_w_down, conv_in_w, conv_w, conv_out_w, pool_w, pool_scale,
           gqa_qkv_w, gqa_qk_norm, gqa_out_w, diff_qkv_w, diff_lambda, diff_subln, diff_out_w):
    assert x.shape == (1, SEQ, D_MODEL) and ctx.shape == (1, CTX_LEN, D_MODEL)
    xs = jnp.concatenate([ctx[0], x[0]], axis=0)

    c_cols = jnp.zeros((D_MODEL, LANES), F32).at[:, 0].set(c_ctx).at[:, 1].set(c[0])
    mods = _modulation(c_cols, mod_w, mod_b)
    mods = mods[:, :2].reshape(DEPTH, 2, N_MOD, 1, D_MODEL)
    mod = lambda layer, m: mods[layer, :, m]

    w_r = jnp.zeros((DEPTH, D_MODEL, LANES), F32)
    w_r = w_r.at[:, :, :N_GROUPS].set(rt_grp_w).at[:, :, N_GROUPS:N_GROUPS + N_EXPERTS].set(rt_exp_w)
    b_r = jnp.zeros((DEPTH, 1, LANES), F32)
    b_r = b_r.at[:, 0, :N_GROUPS].set(rt_grp_b).at[:, 0, N_GROUPS:N_GROUPS + N_EXPERTS].set(rt_exp_b)
    rope = _rope_tables()
    sm_scale = HEAD_DIM ** -0.5 * math.log2(math.e)
    bf = lambda w: w.astype(BF16)

    stack = lambda w: w.reshape((DEPTH * N_EXPERTS,) + w.shape[2:])
    w_gate, w_up, w_down = stack(ex_w_gate), stack(ex_w_up), stack(ex_w_down)

    def moe(layer, x_in, next_mod, kind0=0):
        return _moe_layer(x_in, (mod(layer, 3), mod(layer, 4), mod(layer, 5)), w_r[layer], b_r[layer],
                          w_gate, w_up, w_down, ln_g[layer, 1:2], ln_b[layer, 1:2], next_mod,
                          kind0=kind0, expert0=layer * N_EXPERTS)

    h = _modulate(xs, mod(0, 0), mod(0, 1))
    gb, p = _conv_in(h, bf(conv_in_w[0]))
    r = _conv_mix(p, gb, conv_w[0])
    xs = _mixer_out(r, bf(conv_out_w[0]), xs, mod(0, 2), ln_g[0, 0:1], ln_b[0, 0:1], kind0=0)
    xs, _ = moe(0, xs, None)

    xs = _pool_layer(xs, mod(1, 0), mod(1, 1), mod(1, 2), bf(pool_w[0]), pool_scale[0:1],
                     ln_g[1, 0:1], ln_b[1, 0:1])
    xs, h = moe(1, xs, (mod(2, 0), mod(2, 1)))

    wqkv = bf(gqa_qkv_w[0])
    nq, nkv = GQA_HEADS * HEAD_DIM, GQA_KV_HEADS * HEAD_DIM
    assert PROJ_TN == GQA_GROUP * HEAD_DIM
    qt = _project(h, wqkv, 0, nq, gain=gqa_qk_norm[0, 0:1], rope_tabs=rope, scale=sm_scale, group_heads=True)
    k = _project(h, wqkv, nq, nkv, gain=gqa_qk_norm[0, 1:2], rope_tabs=rope)
    vt = _project(h, wqkv, nq + nkv, nkv, chunk_width=HEAD_DIM)
    o = _gqa_attention(qt, k, vt)
    xs = _mixer_out(o, bf(gqa_out_w[0]), xs, mod(2, 2), ln_g[2, 0:1], ln_b[2, 0:1], kind0=0)
    xs, h = moe(2, xs, (mod(3, 0), mod(3, 1)))

    lam_init = 0.8 - 0.6 * math.exp(-0.3 * 3)
    wqkv = bf(diff_qkv_w[0])
    qt = _project(h, wqkv, 0, D_MODEL, rope_tabs=rope, scale=sm_scale, tile0=1)
    k = _project(h, wqkv, D_MODEL, D_MODEL, rope_tabs=rope)
    vt = _project(h, wqkv, 2 * D_MODEL, D_MODEL, chunk_width=2 * HEAD_DIM)
    o = _diff_attention(qt, k, vt, diff_lambda[0], diff_subln[0:1], lam_init)
    xs = _mixer_out(o, bf(diff_out_w[0]), xs, mod(3, 2), ln_g[3, 0:1], ln_b[3, 0:1], kind0=1)
    out, _ = moe(3, xs, None, kind0=1)
    return out[None]
```

```python
import functools
import math

import jax
import jax.numpy as jnp
from jax import lax
from jax.experimental import pallas as pl
from jax.experimental.pallas import tpu as pltpu

D_MODEL = 2048
SEQ = 8192
DEPTH = 4
GRID_W = 64
CTX_LEN = 256
ALPHA = (2.0 * DEPTH) ** 0.25
ROPE_THETA = 10000.0
LN_EPS = 1e-6
RMS_EPS = 1e-6
N_MOD = 6
HEAD_DIM = 128
GQA_HEADS = D_MODEL // HEAD_DIM
GQA_KV_HEADS = GQA_HEADS // 4
GQA_GROUP = GQA_HEADS // GQA_KV_HEADS
DIFF_HEADS = D_MODEL // (2 * HEAD_DIM)
POOL_WINDOWS = (2, 4, 8, 16)
POOL_GROUP = D_MODEL // len(POOL_WINDOWS)
N_GROUPS = 4
EXPERTS_PER_GROUP = 4
N_EXPERTS = N_GROUPS * EXPERTS_PER_GROUP
EXPERT_FF = 3 * D_MODEL // 8

LANES = 128
SUBLANES = 8
TM = 256
T_ALL = CTX_LEN + SEQ
NT_ALL = T_ALL // TM
HALO = SUBLANES
VMEM_LIMIT = 56 * 1024 * 1024

assert CTX_LEN == TM and SEQ % TM == 0 and DEPTH == 4

F32 = jnp.float32
BF16 = jnp.bfloat16


def _params(*sem):
    return pltpu.CompilerParams(dimension_semantics=sem, vmem_limit_bytes=VMEM_LIMIT)


def _dot(a, b):
    return jnp.dot(a, b, preferred_element_type=F32)


def _layer_norm(y, g, b):
    mu = jnp.mean(y, axis=-1, keepdims=True)
    d = y - mu
    var = jnp.mean(d * d, axis=-1, keepdims=True)
    return d * lax.rsqrt(var + LN_EPS) * g + b


def _mod_spec(kind0):
    return pl.BlockSpec((1, 1, D_MODEL), lambda i, *_: (jnp.minimum(i + kind0, 1), 0, 0))


def _row_spec(width, tile0=0):
    return pl.BlockSpec((TM, width), lambda i, *_: (i + tile0, 0))


def _full_spec(shape):
    nd = len(shape)
    return pl.BlockSpec(shape, lambda i, *_: (0,) * nd)


MOD_TN = 512


def _mod_kernel(c_ref, w_ref, b_ref, o_ref):
    w = w_ref[0]
    rows = []
    for r in range(2):
        c = c_ref[:, r:r + 1]
        s = c * jax.nn.sigmoid(c)
        rows.append(jnp.sum(w * s, axis=0, keepdims=True))
    rows.append(jnp.zeros((SUBLANES - 2, MOD_TN), F32))
    o_ref[0] = jnp.concatenate(rows, axis=0) + b_ref[0]


def _modulation(c_cols, mod_w, mod_b):
    n_out = N_MOD * D_MODEL
    return pl.pallas_call(
        _mod_kernel,
        grid=(DEPTH, n_out // MOD_TN),
        in_specs=[
            pl.BlockSpec((D_MODEL, LANES), lambda l, j: (0, 0)),
            pl.BlockSpec((1, D_MODEL, MOD_TN), lambda l, j: (l, 0, j)),
            pl.BlockSpec((1, 1, MOD_TN), lambda l, j: (l, 0, j)),
        ],
        out_specs=pl.BlockSpec((1, SUBLANES, MOD_TN), lambda l, j: (l, 0, j)),
        out_shape=jax.ShapeDtypeStruct((DEPTH, SUBLANES, n_out), F32),
        compiler_params=_params("arbitrary", "arbitrary"),
        name="modulation",
    )(c_cols, mod_w, mod_b.reshape(DEPTH, 1, n_out))


def _stream_specs(x, tile0=0):
    if isinstance(x, tuple):
        return ([pl.BlockSpec((TM, D_MODEL), lambda i, *_: (0, 0)),
                 pl.BlockSpec((TM, D_MODEL), lambda i, *_: (jnp.maximum(i - 1, 0), 0))], list(x))
    return [_row_spec(D_MODEL, tile0)], [x]


def _stream_tile(x_refs):
    if len(x_refs) == 2:
        return jnp.where(pl.program_id(0) == 0, x_refs[0][...], x_refs[1][...])
    return x_refs[0][...]


def _cast_once(w_ref, w_bf16, first_step):
    @pl.when(first_step)
    def _():
        w_bf16[...] = w_ref[...].astype(BF16)


def _modulate_kernel(*refs):
    *x_refs, sh_ref, sc_ref, o_ref = refs
    o_ref[...] = (_stream_tile(x_refs) * (1.0 + sc_ref[0]) + sh_ref[0]).astype(o_ref.dtype)


def _modulate(x, shift, scale):
    x_specs, x_args = _stream_specs(x)
    return pl.pallas_call(
        _modulate_kernel,
        grid=(NT_ALL,),
        in_specs=x_specs + [_mod_spec(0), _mod_spec(0)],
        out_specs=_row_spec(D_MODEL),
        out_shape=jax.ShapeDtypeStruct((T_ALL, D_MODEL), BF16),
        compiler_params=_params("arbitrary"),
        name="modulate",
    )(*x_args, shift, scale)


CONV_TN = 512


def _conv_in_kernel(h_ref, wb_ref, wc_ref, wv_ref, gb_ref, p_ref, w_bf16):
    for s, w_ref in enumerate((wb_ref, wc_ref, wv_ref)):
        _cast_once(w_ref, w_bf16.at[s], pl.program_id(1) == 0)
    h = h_ref[...]
    gb_ref[...] = _dot(h, w_bf16[0])
    p_ref[...] = _dot(h, w_bf16[1]) * _dot(h, w_bf16[2])


def _conv_in(h, w_in):
    nb = D_MODEL // CONV_TN
    wspec = lambda off: pl.BlockSpec((D_MODEL, CONV_TN), lambda j, i: (0, j + off * nb))
    ospec = pl.BlockSpec((TM, CONV_TN), lambda j, i: (i, j))
    return pl.pallas_call(
        _conv_in_kernel,
        grid=(nb, NT_ALL),
        in_specs=[pl.BlockSpec((TM, D_MODEL), lambda j, i: (i, 0)), wspec(0), wspec(1), wspec(2)],
        out_specs=[ospec, ospec],
        out_shape=[jax.ShapeDtypeStruct((T_ALL, D_MODEL), F32)] * 2,
        scratch_shapes=[pltpu.VMEM((3, D_MODEL, CONV_TN), BF16)],
        compiler_params=_params("arbitrary", "arbitrary"),
        name="conv_in",
    )(h, w_in, w_in, w_in)


def _halo_flags(i, nt):
    has_prev = (i >= 2).astype(F32)
    has_next = jnp.logical_and(i >= 1, i < nt - 1).astype(F32)
    return has_prev, has_next


def _conv_mix_kernel(p_ref, pp_ref, pn_ref, gb_ref, cw_ref, r_ref, ext):
    has_prev, has_next = _halo_flags(pl.program_id(0), NT_ALL)
    ext[0:HALO] = pp_ref[...] * has_prev
    ext[HALO:HALO + TM] = p_ref[...]
    ext[HALO + TM:] = pn_ref[...] * has_next
    q = (ext[HALO - 1:HALO - 1 + TM] * cw_ref[0:1]
         + ext[HALO:HALO + TM] * cw_ref[1:2]
         + ext[HALO + 1:HALO + 1 + TM] * cw_ref[2:3])
    r_ref[...] = (gb_ref[...] * q).astype(r_ref.dtype)


def _halo_specs(width):
    per = TM // HALO
    last = T_ALL // HALO - 1
    prev = pl.BlockSpec((HALO, width), lambda i: (jnp.maximum(i * per - 1, 0), 0))
    nxt = pl.BlockSpec((HALO, width), lambda i: (jnp.minimum((i + 1) * per, last), 0))
    return prev, nxt


def _conv_mix(p, gb, conv_w):
    prev, nxt = _halo_specs(D_MODEL)
    return pl.pallas_call(
        _conv_mix_kernel,
        grid=(NT_ALL,),
        in_specs=[_row_spec(D_MODEL), prev, nxt, _row_spec(D_MODEL), _full_spec((3, D_MODEL))],
        out_specs=_row_spec(D_MODEL),
        out_shape=jax.ShapeDtypeStruct((T_ALL, D_MODEL), BF16),
        scratch_shapes=[pltpu.VMEM((TM + 2 * HALO, D_MODEL), F32)],
        compiler_params=_params("arbitrary"),
        name="conv_mix",
    )(p, p, p, gb, conv_w)


def _mixer_out_kernel(*refs):
    r_ref, w_ref, *x_refs, gate_ref, g_ref, b_ref, o_ref, w_bf16 = refs
    _cast_once(w_ref, w_bf16, pl.program_id(0) == 0)
    o = _dot(r_ref[...], w_bf16[...])
    y = ALPHA * _stream_tile(x_refs) + gate_ref[0] * o
    o_ref[...] = _layer_norm(y, g_ref[...], b_ref[...])


def _mixer_out(r, w_out, x, gate, ln_g, ln_b, *, kind0):
    nt = r.shape[0] // TM
    x_specs, x_args = _stream_specs(x, NT_ALL - nt)
    w_spec = pl.BlockSpec((D_MODEL, D_MODEL), lambda i: (0, 0), pipeline_mode=pl.Buffered(1))
    return pl.pallas_call(
        _mixer_out_kernel,
        grid=(nt,),
        in_specs=[_row_spec(D_MODEL), w_spec] + x_specs
                 + [_mod_spec(kind0), _full_spec((1, D_MODEL)), _full_spec((1, D_MODEL))],
        out_specs=_row_spec(D_MODEL),
        out_shape=jax.ShapeDtypeStruct((nt * TM, D_MODEL), F32),
        scratch_shapes=[pltpu.VMEM((D_MODEL, D_MODEL), BF16)],
        compiler_params=_params("arbitrary"),
        name="mixer_out",
    )(r, w_out, *x_args, gate, ln_g, ln_b)


def _pool_kernel(x_ref, xp_ref, xn_ref, sh_ref, sc_ref, gate_ref, w_ref, ps_ref, g_ref, b_ref, o_ref, ext):
    i = pl.program_id(0)
    has_prev, has_next = _halo_flags(i, NT_ALL)
    one_sc = 1.0 + sc_ref[0]
    sh = sh_ref[0]
    x = x_ref[...]
    h = x * one_sc + sh
    ext[0:HALO] = (xp_ref[...] * one_sc + sh) * has_prev
    ext[HALO:HALO + TM] = h
    ext[HALO + TM:] = (xn_ref[...] * one_sc + sh) * has_next
    t_loc = lax.broadcasted_iota(jnp.int32, (TM, 1), 0) + jnp.maximum(i - 1, 0) * TM
    n_seq = jnp.where(i == 0, CTX_LEN, SEQ)
    mixed = []
    for g, win in enumerate(POOL_WINDOWS):
        c0 = g * POOL_GROUP
        half = win // 2
        s = ext[HALO - half:HALO - half + TM, c0:c0 + POOL_GROUP]
        for j in range(1 - half, win - half):
            s = s + ext[HALO + j:HALO + j + TM, c0:c0 + POOL_GROUP]
        lo = jnp.maximum(t_loc - half, 0)
        hi = jnp.minimum(t_loc + (win - half), n_seq)
        cnt = (hi - lo).astype(F32)
        pooled = s / cnt - h[:, c0:c0 + POOL_GROUP]
        mixed.append(_dot(pooled.astype(BF16), w_ref[g]))
    o = jnp.concatenate(mixed, axis=1) * ps_ref[...]
    y = ALPHA * x + gate_ref[0] * o
    o_ref[...] = _layer_norm(y, g_ref[...], b_ref[...])


def _pool_layer(x, shift, scale, gate, w_grp, pool_scale, ln_g, ln_b):
    prev, nxt = _halo_specs(D_MODEL)
    return pl.pallas_call(
        _pool_kernel,
        grid=(NT_ALL,),
        in_specs=[_row_spec(D_MODEL), prev, nxt, _mod_spec(0), _mod_spec(0), _mod_spec(0),
                  _full_spec((len(POOL_WINDOWS), POOL_GROUP, POOL_GROUP)), _full_spec((1, D_MODEL)),
                  _full_spec((1, D_MODEL)), _full_spec((1, D_MODEL))],
        out_specs=_row_spec(D_MODEL),
        out_shape=jax.ShapeDtypeStruct((T_ALL, D_MODEL), F32),
        scratch_shapes=[pltpu.VMEM((TM + 2 * HALO, D_MODEL), F32)],
        compiler_params=_params("arbitrary"),
        name="pool_layer",
    )(x, x, x, shift, scale, gate, w_grp, pool_scale, ln_g, ln_b)


PROJ_TN = 512


def _proj_kernel(*refs, norm, rope, scale, chunk_width, group_heads):
    refs = list(refs)
    h_ref, w_ref = refs[:2]
    rest = refs[2:]
    gain_ref = rest.pop(0) if norm else None
    if rope:
        c_ref, a_ref, b_ref = rest[:3]
        rest = rest[3:]
    o_ref, w_bf16 = rest
    _cast_once(w_ref, w_bf16, pl.program_id(1) == 0)
    acc = _dot(h_ref[...], w_bf16[...])
    if chunk_width:
        for hh in range(PROJ_TN // chunk_width):
            o_ref[hh, 0] = acc[:, hh * chunk_width:(hh + 1) * chunk_width].T.astype(o_ref.dtype)
        return
    for hh in range(PROJ_TN // HEAD_DIM):
        xh = acc[:, hh * HEAD_DIM:(hh + 1) * HEAD_DIM]
        if norm:
            ms = jnp.mean(xh * xh, axis=-1, keepdims=True)
            xh = xh * lax.rsqrt(ms + RMS_EPS) * gain_ref[...]
        if rope:
            xh = (xh * c_ref[...] + pltpu.roll(xh, HEAD_DIM - 32, axis=1) * a_ref[...]
                  + pltpu.roll(xh, 32, axis=1) * b_ref[...])
        if scale != 1.0:
            xt = (xh * scale).T.astype(o_ref.dtype)
            if group_heads:
                o_ref[0, 0, :, hh * TM:(hh + 1) * TM] = xt
            else:
                o_ref[hh * HEAD_DIM:(hh + 1) * HEAD_DIM, :] = xt
        else:
            o_ref[:, hh * HEAD_DIM:(hh + 1) * HEAD_DIM] = xh.astype(o_ref.dtype)


def _project(h, w, col0, width, *, gain=None, rope_tabs=None, scale=1.0, tile0=0, chunk_width=0,
             group_heads=False):
    nt = NT_ALL - tile0
    nb = width // PROJ_TN
    off = col0 // PROJ_TN
    in_specs = [pl.BlockSpec((TM, D_MODEL), lambda j, i: (i + tile0, 0)),
                pl.BlockSpec((D_MODEL, PROJ_TN), lambda j, i: (0, j + off))]
    args = [h, w]
    if gain is not None:
        in_specs.append(pl.BlockSpec((1, HEAD_DIM), lambda j, i: (0, 0)))
        args.append(gain)
    if rope_tabs is not None:
        in_specs += [pl.BlockSpec((TM, HEAD_DIM), lambda j, i: (i + tile0, 0))] * 3
        args += list(rope_tabs)
    if chunk_width:
        per = PROJ_TN // chunk_width
        out_spec = pl.BlockSpec((per, 1, chunk_width, TM), lambda j, i: (j, i, 0, 0))
        out_shape = jax.ShapeDtypeStruct((width // chunk_width, nt, chunk_width, TM), BF16)
    elif scale != 1.0 and group_heads:
        per = PROJ_TN // HEAD_DIM
        out_spec = pl.BlockSpec((1, 1, HEAD_DIM, per * TM), lambda j, i: (j, i, 0, 0))
        out_shape = jax.ShapeDtypeStruct((nb, nt, HEAD_DIM, per * TM), BF16)
    elif scale != 1.0:
        out_spec = pl.BlockSpec((PROJ_TN, TM), lambda j, i: (j, i))
        out_shape = jax.ShapeDtypeStruct((width, nt * TM), BF16)
    else:
        out_spec = pl.BlockSpec((TM, PROJ_TN), lambda j, i: (i, j))
        out_shape = jax.ShapeDtypeStruct((nt * TM, width), BF16)
    kern = functools.partial(_proj_kernel, norm=gain is not None, rope=rope_tabs is not None, scale=scale,
                             chunk_width=chunk_width, group_heads=group_heads)
    return pl.pallas_call(
        kern,
        grid=(nb, nt),
        in_specs=in_specs,
        out_specs=out_spec,
        out_shape=out_shape,
        scratch_shapes=[pltpu.VMEM((D_MODEL, PROJ_TN), BF16)],
        compiler_params=_params("arbitrary", "arbitrary"),
        name="project",
    )(*args)


def _softmax_update(s, vt_c, m_ref, l_ref, acc_ref):
    m_prev = m_ref[...]
    m_new = jnp.maximum(m_prev, jnp.max(s, axis=0, keepdims=True))
    p = jnp.exp2(s - m_new)
    alpha = jnp.exp2(m_prev - m_new)
    l_ref[...] = alpha * l_ref[...] + jnp.sum(p, axis=0, keepdims=True)
    acc_ref[...] = alpha * acc_ref[...] + _dot(vt_c, p.astype(BF16))
    m_ref[...] = m_new


def _key_tile(k_ref, j):
    return k_ref[j * TM:(j + 1) * TM, :]


def _softmax_pipeline(steps, m_ref, l_ref, acc_ref):
    s_next = steps[0][0]()
    for i, (_, vt_c, stream) in enumerate(steps):
        s_cur = s_next
        if i + 1 < len(steps):
            s_next = steps[i + 1][0]()
        _softmax_update(s_cur, vt_c, m_ref.at[stream], l_ref.at[stream], acc_ref.at[stream])


def _attend(make_steps, context_queries, m_ref, l_ref, acc_ref):
    m_ref[...] = jnp.full(m_ref.shape, -jnp.inf, F32)
    l_ref[...] = jnp.zeros(l_ref.shape, F32)
    acc_ref[...] = jnp.zeros(acc_ref.shape, F32)

    if context_queries is not None:
        @pl.when(context_queries)
        def _():
            _softmax_pipeline(make_steps(0), m_ref, l_ref, acc_ref)

    def all_tiles():
        steps = []
        for j in range(NT_ALL):
            steps += make_steps(j)
        _softmax_pipeline(steps, m_ref, l_ref, acc_ref)

    if context_queries is None:
        all_tiles()
    else:
        pl.when(jnp.logical_not(context_queries))(all_tiles)


def _gqa_kernel(qt_ref, k_ref, vt_ref, o_ref, m_ref, l_ref, acc_ref):
    wide = GQA_STREAM_HEADS * TM

    def make_steps(j):
        k_c = _key_tile(k_ref, j)
        vt_c = vt_ref[0, j]
        return [(functools.partial(lambda s: _dot(k_c, qt_ref[0, 0, :, s * wide:(s + 1) * wide]), s), vt_c, s)
                for s in range(GQA_GROUP // GQA_STREAM_HEADS)]
    _attend(make_steps, pl.program_id(1) == 0, m_ref, l_ref, acc_ref)

    for g in range(GQA_GROUP):
        s, c = divmod(g, GQA_STREAM_HEADS)
        o = acc_ref[s, :, c * TM:(c + 1) * TM] / l_ref[s, :, c * TM:(c + 1) * TM]
        o_ref[:, g * HEAD_DIM:(g + 1) * HEAD_DIM] = o.T.astype(o_ref.dtype)


GQA_STREAM_HEADS = 2


def _gqa_attention(qt, k, vt):
    wide = GQA_GROUP * TM
    streams = GQA_GROUP // GQA_STREAM_HEADS
    sw = GQA_STREAM_HEADS * TM
    return pl.pallas_call(
        _gqa_kernel,
        grid=(GQA_KV_HEADS, NT_ALL),
        in_specs=[pl.BlockSpec((1, 1, HEAD_DIM, wide), lambda kv, i: (kv, i, 0, 0)),
                  pl.BlockSpec((T_ALL, HEAD_DIM), lambda kv, i: (0, kv)),
                  pl.BlockSpec((1, NT_ALL, HEAD_DIM, TM), lambda kv, i: (kv, 0, 0, 0))],
        out_specs=pl.BlockSpec((TM, GQA_GROUP * HEAD_DIM), lambda kv, i: (i, kv)),
        out_shape=jax.ShapeDtypeStruct((T_ALL, D_MODEL), BF16),
        scratch_shapes=[pltpu.VMEM((streams, 1, sw), F32), pltpu.VMEM((streams, 1, sw), F32),
                        pltpu.VMEM((streams, HEAD_DIM, sw), F32)],
        compiler_params=_params("arbitrary", "arbitrary"),
        name="gqa_attention",
    )(qt, k, vt)


def _diff_kernel(qt_ref, k_ref, vt_ref, lam_ref, sub_ref, o_ref, m_ref, l_ref, acc_ref, *, lam_init):
    def make_steps(j):
        k_c = _key_tile(k_ref, j)
        vt_c = vt_ref[0, j]
        return [(functools.partial(lambda s: _dot(k_c[:, s * HEAD_DIM:(s + 1) * HEAD_DIM],
                                                  qt_ref[s * HEAD_DIM:(s + 1) * HEAD_DIM, :]), s), vt_c, s)
                for s in range(2)]
    _attend(make_steps, None, m_ref, l_ref, acc_ref)

    lv = lam_ref[...]
    lam = (jnp.exp(jnp.sum(lv[0:1] * lv[1:2], axis=-1, keepdims=True))
           - jnp.exp(jnp.sum(lv[2:3] * lv[3:4], axis=-1, keepdims=True)) + lam_init)
    o = (acc_ref[0] / l_ref[0] - lam * (acc_ref[1] / l_ref[1])).T
    ms = jnp.mean(o * o, axis=-1, keepdims=True)
    o = o * lax.rsqrt(ms + RMS_EPS) * sub_ref[...] * (1.0 - lam_init)
    o_ref[...] = o.astype(o_ref.dtype)


def _diff_attention(qt, k, vt, lam_vecs, subln, lam_init):
    wv = 2 * HEAD_DIM
    return pl.pallas_call(
        functools.partial(_diff_kernel, lam_init=lam_init),
        grid=(DIFF_HEADS, SEQ // TM),
        in_specs=[pl.BlockSpec((wv, TM), lambda hd, i: (hd, i)),
                  pl.BlockSpec((T_ALL, wv), lambda hd, i: (0, hd)),
                  pl.BlockSpec((1, NT_ALL, wv, TM), lambda hd, i: (hd, 0, 0, 0)),
                  pl.BlockSpec((4, HEAD_DIM), lambda hd, i: (0, 0)), pl.BlockSpec((1, wv), lambda hd, i: (0, 0))],
        out_specs=pl.BlockSpec((TM, wv), lambda hd, i: (i, hd)),
        out_shape=jax.ShapeDtypeStruct((SEQ, D_MODEL), BF16),
        scratch_shapes=[pltpu.VMEM((2, 1, TM), F32), pltpu.VMEM((2, 1, TM), F32), pltpu.VMEM((2, wv, TM), F32)],
        compiler_params=_params("arbitrary", "arbitrary"),
        name="diff_attention",
    )(qt, k, vt, lam_vecs, subln)


def _lane_pick(vals, lane, idx):
    return jnp.sum(jnp.where(lane == idx, vals, 0.0), axis=-1, keepdims=True)


def _router_kernel(x_ref, sh_ref, sc_ref, w_ref, b_ref, meta_ref, cnt_ref, carry):
    i = pl.program_id(0)

    @pl.when(i == 0)
    def _():
        carry[...] = jnp.zeros(carry.shape, F32)

    h = x_ref[...] * (1.0 + sc_ref[0]) + sh_ref[0]
    h_hi = h.astype(BF16)
    h_lo = (h - h_hi.astype(F32)).astype(BF16)
    hw = _dot(h_hi, w_ref[...])
    logits = hw[:, 0:LANES] + hw[:, LANES:2 * LANES] + _dot(h_lo, w_ref[:, 0:LANES]) + b_ref[...]
    lane = lax.broadcasted_iota(jnp.int32, (TM, LANES), 1).astype(F32)
    neg = -jnp.inf
    big = float(LANES)

    grp = jnp.where(lane < N_GROUPS, logits, neg)
    gmax = jnp.max(grp, axis=-1, keepdims=True)
    gstar = jnp.min(jnp.where(grp == gmax, lane, big), axis=-1, keepdims=True)
    p_group = 1.0 / jnp.sum(jnp.exp(grp - gmax), axis=-1, keepdims=True)

    e_lo = N_GROUPS + EXPERTS_PER_GROUP * gstar
    own = jnp.logical_and(lane >= e_lo, lane < e_lo + EXPERTS_PER_GROUP)
    le = jnp.where(own, logits, neg)
    v1 = jnp.max(le, axis=-1, keepdims=True)
    i1 = jnp.min(jnp.where(le == v1, lane, big), axis=-1, keepdims=True)
    le2 = jnp.where(lane == i1, neg, le)
    v2 = jnp.max(le2, axis=-1, keepdims=True)
    i2 = jnp.min(jnp.where(le2 == v2, lane, big), axis=-1, keepdims=True)
    t2 = jnp.exp(v2 - v1)
    w1 = p_group / (1.0 + t2)
    w2 = p_group * t2 / (1.0 + t2)
    e1 = i1 - N_GROUPS
    e2 = i2 - N_GROUPS

    chosen = jnp.logical_or(lane == e1, lane == e2)
    onehot = jnp.where(chosen, 1.0, 0.0)
    r_io = lax.broadcasted_iota(jnp.int32, (TM, TM), 0)
    c_io = lax.broadcasted_iota(jnp.int32, (TM, TM), 1)
    tri = jnp.where(c_io < r_io, 1.0, 0.0).astype(BF16)
    before = carry[...] + _dot(tri, onehot.astype(BF16))
    r1 = _lane_pick(before, lane, e1)
    r2 = _lane_pick(before, lane, e2)
    carry[...] = carry[...] + jnp.sum(onehot, axis=0, keepdims=True)

    meta = jnp.where(lane == 0, e1, 0.0)
    for col, val in ((1, e2), (2, r1), (3, r2), (4, w1), (5, w2)):
        meta = jnp.where(lane == col, val, meta)
    meta_ref[...] = meta
    cnt_ref[...] = jnp.broadcast_to(carry[...], cnt_ref.shape)


def _router(x, shift, scale, w_r, b_r, *, kind0):
    nt = x.shape[0] // TM
    return pl.pallas_call(
        _router_kernel,
        grid=(nt,),
        in_specs=[_row_spec(D_MODEL), _mod_spec(kind0), _mod_spec(kind0),
                  _full_spec((D_MODEL, 2 * LANES)), _full_spec((1, LANES))],
        out_specs=[_row_spec(LANES), _full_spec((SUBLANES, LANES))],
        out_shape=[jax.ShapeDtypeStruct((nt * TM, LANES), F32), jax.ShapeDtypeStruct((SUBLANES, LANES), F32)],
        scratch_shapes=[pltpu.VMEM((1, LANES), F32)],
        compiler_params=_params("arbitrary"),
        name="router",
    )(x, shift, scale, w_r, b_r)


def _row_copy(src, src_row, dst, dst_row, sem):
    return pltpu.make_async_copy(src.at[pl.ds(src_row, 1)], dst.at[pl.ds(dst_row, 1)], sem)


def _dispatch_kernel(pos_ref, x_ref, sh_ref, sc_ref, init_ref, out_ref, hbuf, sem, *, n_tok):
    del init_ref
    i = pl.program_id(0)
    hbuf[...] = x_ref[...] * (1.0 + sc_ref[0]) + sh_ref[0]

    def start(r, carry):
        for k in range(2):
            _row_copy(hbuf, r, out_ref, pos_ref[k * n_tok + i * TM + r], sem).start()
        return carry
    lax.fori_loop(0, TM, start, 0, unroll=ROW_DMA_UNROLL)
    for k in range(2):
        _tile_wait(hbuf, out_ref, sem)


ROW_DMA_UNROLL = 8


def _tile_wait(vmem_tile, hbm, sem):
    pltpu.make_async_copy(hbm.at[pl.ds(0, TM)], vmem_tile, sem).wait()


def _dispatch(pos, x, shift, scale, n_sorted, *, kind0):
    nt = x.shape[0] // TM
    init = jnp.zeros((n_sorted, D_MODEL), F32)
    return pl.pallas_call(
        functools.partial(_dispatch_kernel, n_tok=x.shape[0]),
        grid_spec=pltpu.PrefetchScalarGridSpec(
            num_scalar_prefetch=1,
            grid=(nt,),
            in_specs=[_row_spec(D_MODEL), _mod_spec(kind0), _mod_spec(kind0),
                      pl.BlockSpec(memory_space=pl.ANY)],
            out_specs=pl.BlockSpec(memory_space=pl.ANY),
            scratch_shapes=[pltpu.VMEM((TM, D_MODEL), F32), pltpu.SemaphoreType.DMA],
        ),
        out_shape=jax.ShapeDtypeStruct((n_sorted, D_MODEL), F32),
        input_output_aliases={4: 0},
        compiler_params=_params("arbitrary"),
        name="moe_dispatch",
    )(pos, x, shift, scale, init)


PLAN_EXPERT = 0
PLAN_FIRST = 1
PLAN_SLOT = 2
PLAN_NEXT = 3
PLAN_ROWS_VALID = 4
PLAN_SRC_TILE = 5
PLAN_ROWS = 6


def _ffn_kernel(plan_ref, a_ref, wg_hbm, wu_hbm, wd_hbm, o_ref, wg_buf, wu_buf, wd_buf, sems):
    i = pl.program_id(0)
    slot = plan_ref[PLAN_SLOT, i]
    rows = plan_ref[PLAN_ROWS_VALID, i]

    def weight_copies(expert, s):
        return [pltpu.make_async_copy(hbm.at[expert], buf.at[s], sems.at[s, j])
                for j, (hbm, buf) in enumerate(((wg_hbm, wg_buf), (wu_hbm, wu_buf), (wd_hbm, wd_buf)))]

    @pl.when(i == 0)
    def _():
        for c in weight_copies(plan_ref[PLAN_EXPERT, 0], 0):
            c.start()

    @pl.when(plan_ref[PLAN_FIRST, i] == 1)
    def _():
        for c in weight_copies(plan_ref[PLAN_EXPERT, i], slot):
            c.wait()
        nxt = plan_ref[PLAN_NEXT, i]

        @pl.when(nxt >= 0)
        def _():
            for c in weight_copies(nxt, 1 - slot):
                c.start()

    @pl.when(rows > 0)
    def _():
        a = a_ref[...].astype(BF16)
        g = _dot(a, wg_buf[slot].astype(BF16))
        u = _dot(a, wu_buf[slot].astype(BF16))
        act = (g * jax.nn.sigmoid(g) * u).astype(BF16)
        o_ref[...] = _dot(act, wd_buf[slot].astype(BF16))

    @pl.when(rows == 0)
    def _():
        o_ref[...] = jnp.zeros(o_ref.shape, F32)


def _expert_ffn(plan, a_sorted, w_gate, w_up, w_down):
    n_tiles = a_sorted.shape[0] // TM
    return pl.pallas_call(
        _ffn_kernel,
        grid_spec=pltpu.PrefetchScalarGridSpec(
            num_scalar_prefetch=1,
            grid=(n_tiles,),
            in_specs=[pl.BlockSpec((TM, D_MODEL), lambda i, plan: (plan[PLAN_SRC_TILE, i], 0)),
                      pl.BlockSpec(memory_space=pl.ANY), pl.BlockSpec(memory_space=pl.ANY),
                      pl.BlockSpec(memory_space=pl.ANY)],
            out_specs=pl.BlockSpec((TM, D_MODEL), lambda i, plan: (i, 0)),
            scratch_shapes=[pltpu.VMEM((2, D_MODEL, EXPERT_FF), F32), pltpu.VMEM((2, D_MODEL, EXPERT_FF), F32),
                            pltpu.VMEM((2, EXPERT_FF, D_MODEL), F32), pltpu.SemaphoreType.DMA((2, 3))],
        ),
        out_shape=jax.ShapeDtypeStruct(a_sorted.shape, F32),
        compiler_params=_params("arbitrary"),
        name="moe_experts",
    )(plan, a_sorted, w_gate, w_up, w_down)


def _combine_kernel(pos_ref, y_ref, x_ref, meta_ref, gate_ref, g_ref, b_ref, *rest, emit_h, nt):
    if emit_h:
        nsh_ref, nsc_ref, o_ref, h_ref, bufs, sems = rest
    else:
        o_ref, bufs, sems = rest
    i = pl.program_id(0)

    def gather(tile, slot):
        def start(r, carry):
            for k in range(2):
                _row_copy(y_ref, pos_ref[k * (nt * TM) + tile * TM + r], bufs.at[slot, k], r, sems.at[slot]).start()
            return carry
        lax.fori_loop(0, TM, start, 0, unroll=ROW_DMA_UNROLL)

    @pl.when(i == 0)
    def _():
        gather(0, 0)

    @pl.when(i + 1 < nt)
    def _():
        gather(i + 1, (i + 1) & 1)

    slot = i & 1
    for k in range(2):
        _tile_wait(bufs.at[slot, k], y_ref, sems.at[slot])

    meta = meta_ref[...]
    moe = meta[:, 4:5] * bufs[slot, 0] + meta[:, 5:6] * bufs[slot, 1]
    y = ALPHA * x_ref[...] + gate_ref[0] * moe
    out = _layer_norm(y, g_ref[...], b_ref[...])
    o_ref[...] = out
    if emit_h:
        h_ref[...] = (out * (1.0 + nsc_ref[0]) + nsh_ref[0]).astype(h_ref.dtype)


def _combine(pos, y_sorted, x, meta, gate, ln_g, ln_b, next_mod, *, kind0):
    nt = x.shape[0] // TM
    emit_h = next_mod is not None
    mspec = _mod_spec(kind0)
    vec = _full_spec((1, D_MODEL))
    ospec = _row_spec(D_MODEL)
    in_specs = [pl.BlockSpec(memory_space=pl.ANY), ospec, _row_spec(LANES), mspec, vec, vec]
    args = [pos, y_sorted, x, meta, gate, ln_g, ln_b]
    out_specs = [ospec]
    out_shape = [jax.ShapeDtypeStruct((nt * TM, D_MODEL), F32)]
    if emit_h:
        in_specs += [mspec, mspec]
        args += list(next_mod)
        out_specs.append(ospec)
        out_shape.append(jax.ShapeDtypeStruct((nt * TM, D_MODEL), BF16))
    res = pl.pallas_call(
        functools.partial(_combine_kernel, emit_h=emit_h, nt=nt),
        grid_spec=pltpu.PrefetchScalarGridSpec(
            num_scalar_prefetch=1,
            grid=(nt,),
            in_specs=in_specs,
            out_specs=out_specs,
            scratch_shapes=[pltpu.VMEM((2, 2, TM, D_MODEL), F32), pltpu.SemaphoreType.DMA((2,))],
        ),
        out_shape=out_shape,
        compiler_params=_params("arbitrary"),
        name="moe_combine",
    )(*args)
    return res if emit_h else (res[0], None)


def _moe_layer(x, mods, w_r, b_r, w_gate, w_up, w_down, ln_g, ln_b, next_mod, *, kind0, expert0):
    shift, scale, gate = mods
    n_tok = x.shape[0]
    n_tiles = -(-(2 * n_tok + N_EXPERTS * (TM - 1)) // TM)
    meta, cnt = _router(x, shift, scale, w_r, b_r, kind0=kind0)

    i32 = jnp.int32
    counts = cnt[0, :N_EXPERTS].astype(i32)
    padded = ((counts + TM - 1) // TM) * TM
    ends = jnp.cumsum(padded)
    offs = ends - padded
    meta_t = meta[:, :SUBLANES].T
    experts, ranks = meta_t[0:2].astype(i32), meta_t[2:4].astype(i32)
    ids = jnp.arange(N_EXPERTS, dtype=i32)
    own_offs = jnp.sum(jnp.where(experts[..., None] == ids, offs, 0), axis=-1)
    pos = (own_offs + ranks).reshape(-1)

    n_valid = ends[-1] // TM
    tiles = jnp.arange(n_tiles, dtype=i32)
    tile_start = tiles * TM
    tile_valid = tile_start < ends[-1]
    te = jnp.sum(tile_start[:, None] >= ends[None, :], axis=1).astype(i32)
    te_last = jnp.sum((n_valid - 1) * TM >= ends).astype(i32)
    te = jnp.where(tile_valid, te, te_last)
    nonempty = counts > 0
    run = jnp.cumsum(nonempty.astype(i32)) - 1
    later = jnp.where(nonempty[None, :] & (ids[None, :] > ids[:, None]), ids[None, :], N_EXPERTS)
    nxt = jnp.min(later, axis=1)
    nxt = jnp.where(nxt < N_EXPERTS, nxt + expert0, -1)
    plan = jnp.stack([
        te + expert0,
        (tile_valid & (tile_start == offs[te])).astype(i32),
        run[te] & 1,
        nxt[te],
        jnp.where(tile_valid, jnp.clip(counts[te] - (tile_start - offs[te]), 0, TM), 0),
        jnp.minimum(tiles, n_valid - 1),
    ]).astype(i32)
    assert plan.shape[0] == PLAN_ROWS

    a_sorted = _dispatch(pos, x, shift, scale, n_tiles * TM, kind0=kind0)
    y_sorted = _expert_ffn(plan, a_sorted, w_gate, w_up, w_down)
    return _combine(pos, y_sorted, x, meta, gate, ln_g, ln_b, next_mod, kind0=kind0)


def _rope_tables():
    rows = SEQ // GRID_W
    axis_dim = HEAD_DIM // 2
    inv = ROPE_THETA ** (-jnp.arange(0, axis_dim, 2, dtype=F32) / axis_dim)
    row_ang = jnp.arange(rows, dtype=F32)[:, None] * inv
    col_ang = jnp.arange(GRID_W, dtype=F32)[:, None] * inv
    cr, sr = (jnp.repeat(f(row_ang), GRID_W, axis=0) for f in (jnp.cos, jnp.sin))
    cc, sc = (jnp.tile(f(col_ang), (rows, 1)) for f in (jnp.cos, jnp.sin))
    z = jnp.zeros_like(sr)
    c_tab = jnp.concatenate([cr, cr, cc, cc], axis=1)
    a_tab = jnp.concatenate([-sr, z, -sc, z], axis=1)
    b_tab = jnp.concatenate([z, sr, z, sc], axis=1)
    ctx = (jnp.ones((CTX_LEN, HEAD_DIM), F32), jnp.zeros((CTX_LEN, HEAD_DIM), F32), jnp.zeros((CTX_LEN, HEAD_DIM), F32))
    return tuple(jnp.concatenate([c0, t], axis=0) for c0, t in zip(ctx, (c_tab, a_tab, b_tab)))


def kernel(x, c, ctx, c_ctx, mod_w, mod_b, ln_g, ln_b, rt_grp_w, rt_grp_b, rt_exp_w, rt_exp_b,
           ex_w_gate, ex_w_up, ex_w_down, conv_in_w, conv_w, conv_out_w, pool_w, pool_scale,
           gqa_qkv_w, gqa_qk_norm, gqa_out_w, diff_qkv_w, diff_lambda, diff_subln, diff_out_w):
    assert x.shape == (1, SEQ, D_MODEL) and ctx.shape == (1, CTX_LEN, D_MODEL)
    xs = (ctx[0], x[0])

    c_cols = jnp.pad(jnp.stack([c_ctx, c[0]], axis=1), ((0, 0), (0, LANES - 2)))
    mods = _modulation(c_cols, mod_w, mod_b)
    mods = mods[:, :2].reshape(DEPTH, 2, N_MOD, 1, D_MODEL)
    mod = lambda layer, m: mods[layer, :, m]

    n_pad = LANES - N_GROUPS - N_EXPERTS
    w_r = jnp.concatenate([rt_grp_w, rt_exp_w, jnp.zeros((DEPTH, D_MODEL, n_pad), F32)], axis=-1)
    w_hi = w_r.astype(BF16)
    w_r = jnp.concatenate([w_hi, (w_r - w_hi.astype(F32)).astype(BF16)], axis=-1)
    b_r = jnp.concatenate([rt_grp_b, rt_exp_b, jnp.zeros((DEPTH, n_pad), F32)], axis=-1)[:, None, :]
    rope = _rope_tables()
    sm_scale = HEAD_DIM ** -0.5 * math.log2(math.e)
    bf = lambda w: w.astype(BF16)

    stack = lambda w: w.reshape((DEPTH * N_EXPERTS,) + w.shape[2:])
    w_gate, w_up, w_down = stack(ex_w_gate), stack(ex_w_up), stack(ex_w_down)

    def moe(layer, x_in, next_mod, kind0=0):
        return _moe_layer(x_in, (mod(layer, 3), mod(layer, 4), mod(layer, 5)), w_r[layer], b_r[layer],
                          w_gate, w_up, w_down, ln_g[layer, 1:2], ln_b[layer, 1:2], next_mod,
                          kind0=kind0, expert0=layer * N_EXPERTS)

    h = _modulate(xs, mod(0, 0), mod(0, 1))
    gb, p = _conv_in(h, conv_in_w[0])
    r = _conv_mix(p, gb, conv_w[0])
    xs = _mixer_out(r, conv_out_w[0], xs, mod(0, 2), ln_g[0, 0:1], ln_b[0, 0:1], kind0=0)
    xs, _ = moe(0, xs, None)

    xs = _pool_layer(xs, mod(1, 0), mod(1, 1), mod(1, 2), bf(pool_w[0]), pool_scale[0:1],
                     ln_g[1, 0:1], ln_b[1, 0:1])
    xs, h = moe(1, xs, (mod(2, 0), mod(2, 1)))

    wqkv = gqa_qkv_w[0]
    nq, nkv = GQA_HEADS * HEAD_DIM, GQA_KV_HEADS * HEAD_DIM
    assert PROJ_TN == GQA_GROUP * HEAD_DIM
    qt = _project(h, wqkv, 0, nq, gain=gqa_qk_norm[0, 0:1], rope_tabs=rope, scale=sm_scale, group_heads=True)
    k = _project(h, wqkv, nq, nkv, gain=gqa_qk_norm[0, 1:2], rope_tabs=rope)
    vt = _project(h, wqkv, nq + nkv, nkv, chunk_width=HEAD_DIM)
    o = _gqa_attention(qt, k, vt)
    xs = _mixer_out(o, gqa_out_w[0], xs, mod(2, 2), ln_g[2, 0:1], ln_b[2, 0:1], kind0=0)
    xs, h = moe(2, xs, (mod(3, 0), mod(3, 1)))

    lam_init = 0.8 - 0.6 * math.exp(-0.3 * 3)
    wqkv = diff_qkv_w[0]
    qt = _project(h, wqkv, 0, D_MODEL, rope_tabs=rope, scale=sm_scale, tile0=1)
    k = _project(h, wqkv, D_MODEL, D_MODEL, rope_tabs=rope)
    vt = _project(h, wqkv, 2 * D_MODEL, D_MODEL, chunk_width=2 * HEAD_DIM)
    o = _diff_attention(qt, k, vt, diff_lambda[0], diff_subln[0:1], lam_init)
    xs = _mixer_out(o, diff_out_w[0], xs, mod(3, 2), ln_g[3, 0:1], ln_b[3, 0:1], kind0=1)
    out, _ = moe(3, xs, None, kind0=1)
    return out[None]
```

```python
import functools
import math

import jax
import jax.numpy as jnp
from jax import lax
from jax.experimental import pallas as pl
from jax.experimental.pallas import tpu as pltpu

D_MODEL = 2048
SEQ = 8192
DEPTH = 4
GRID_W = 64
CTX_LEN = 256
ALPHA = (2.0 * DEPTH) ** 0.25
ROPE_THETA = 10000.0
LN_EPS = 1e-6
RMS_EPS = 1e-6
N_MOD = 6
HEAD_DIM = 128
GQA_HEADS = D_MODEL // HEAD_DIM
GQA_KV_HEADS = GQA_HEADS // 4
GQA_GROUP = GQA_HEADS // GQA_KV_HEADS
DIFF_HEADS = D_MODEL // (2 * HEAD_DIM)
POOL_WINDOWS = (2, 4, 8, 16)
POOL_GROUP = D_MODEL // len(POOL_WINDOWS)
N_GROUPS = 4
EXPERTS_PER_GROUP = 4
N_EXPERTS = N_GROUPS * EXPERTS_PER_GROUP
EXPERT_FF = 3 * D_MODEL // 8

LANES = 128
SUBLANES = 8
TM = 256
T_ALL = CTX_LEN + SEQ
NT_ALL = T_ALL // TM
HALO = SUBLANES
VMEM_LIMIT = 56 * 1024 * 1024

assert CTX_LEN == TM and SEQ % TM == 0 and DEPTH == 4

F32 = jnp.float32
BF16 = jnp.bfloat16


def _params(*sem):
    return pltpu.CompilerParams(dimension_semantics=sem, vmem_limit_bytes=VMEM_LIMIT)


def _dot(a, b):
    return jnp.dot(a, b, preferred_element_type=F32)


def _layer_norm(y, g, b):
    mu = jnp.mean(y, axis=-1, keepdims=True)
    d = y - mu
    var = jnp.mean(d * d, axis=-1, keepdims=True)
    return d * lax.rsqrt(var + LN_EPS) * g + b


def _mod_spec(kind0):
    return pl.BlockSpec((1, 1, D_MODEL), lambda i, *_: (jnp.minimum(i + kind0, 1), 0, 0))


def _row_spec(width, tile0=0):
    return pl.BlockSpec((TM, width), lambda i, *_: (i + tile0, 0))


def _full_spec(shape):
    nd = len(shape)
    return pl.BlockSpec(shape, lambda i, *_: (0,) * nd)


MOD_TN = 512


def _mod_kernel(c_ref, w_ref, b_ref, o_ref):
    w = w_ref[0]
    rows = []
    for r in range(2):
        c = c_ref[:, r:r + 1]
        s = c * jax.nn.sigmoid(c)
        rows.append(jnp.sum(w * s, axis=0, keepdims=True))
    rows.append(jnp.zeros((SUBLANES - 2, MOD_TN), F32))
    o_ref[0] = jnp.concatenate(rows, axis=0) + b_ref[0]


def _modulation(c_cols, mod_w, mod_b):
    n_out = N_MOD * D_MODEL
    return pl.pallas_call(
        _mod_kernel,
        grid=(DEPTH, n_out // MOD_TN),
        in_specs=[
            pl.BlockSpec((D_MODEL, LANES), lambda l, j: (0, 0)),
            pl.BlockSpec((1, D_MODEL, MOD_TN), lambda l, j: (l, 0, j)),
            pl.BlockSpec((1, 1, MOD_TN), lambda l, j: (l, 0, j)),
        ],
        out_specs=pl.BlockSpec((1, SUBLANES, MOD_TN), lambda l, j: (l, 0, j)),
        out_shape=jax.ShapeDtypeStruct((DEPTH, SUBLANES, n_out), F32),
        compiler_params=_params("arbitrary", "arbitrary"),
        name="modulation",
    )(c_cols, mod_w, mod_b.reshape(DEPTH, 1, n_out))


def _stream_specs(x, tile0=0):
    if isinstance(x, tuple):
        return ([pl.BlockSpec((TM, D_MODEL), lambda i, *_: (0, 0)),
                 pl.BlockSpec((TM, D_MODEL), lambda i, *_: (jnp.maximum(i - 1, 0), 0))], list(x))
    return [_row_spec(D_MODEL, tile0)], [x]


def _stream_tile(x_refs):
    if len(x_refs) == 2:
        return jnp.where(pl.program_id(0) == 0, x_refs[0][...], x_refs[1][...])
    return x_refs[0][...]


def _cast_once(w_ref, w_bf16, first_step):
    @pl.when(first_step)
    def _():
        w_bf16[...] = w_ref[...].astype(BF16)


def _modulate_kernel(*refs):
    *x_refs, sh_ref, sc_ref, o_ref = refs
    o_ref[...] = (_stream_tile(x_refs) * (1.0 + sc_ref[0]) + sh_ref[0]).astype(o_ref.dtype)


def _modulate(x, shift, scale):
    x_specs, x_args = _stream_specs(x)
    return pl.pallas_call(
        _modulate_kernel,
        grid=(NT_ALL,),
        in_specs=x_specs + [_mod_spec(0), _mod_spec(0)],
        out_specs=_row_spec(D_MODEL),
        out_shape=jax.ShapeDtypeStruct((T_ALL, D_MODEL), BF16),
        compiler_params=_params("arbitrary"),
        name="modulate",
    )(*x_args, shift, scale)


CONV_TN = 512


def _conv_in_kernel(h_ref, wb_ref, wc_ref, wv_ref, gb_ref, p_ref, w_bf16):
    for s, w_ref in enumerate((wb_ref, wc_ref, wv_ref)):
        _cast_once(w_ref, w_bf16.at[s], pl.program_id(1) == 0)
    h = h_ref[...]
    gb_ref[...] = _dot(h, w_bf16[0])
    p_ref[...] = _dot(h, w_bf16[1]) * _dot(h, w_bf16[2])


def _conv_in(h, w_in):
    nb = D_MODEL // CONV_TN
    wspec = lambda off: pl.BlockSpec((D_MODEL, CONV_TN), lambda j, i: (0, j + off * nb))
    ospec = pl.BlockSpec((TM, CONV_TN), lambda j, i: (i, j))
    return pl.pallas_call(
        _conv_in_kernel,
        grid=(nb, NT_ALL),
        in_specs=[pl.BlockSpec((TM, D_MODEL), lambda j, i: (i, 0)), wspec(0), wspec(1), wspec(2)],
        out_specs=[ospec, ospec],
        out_shape=[jax.ShapeDtypeStruct((T_ALL, D_MODEL), F32)] * 2,
        scratch_shapes=[pltpu.VMEM((3, D_MODEL, CONV_TN), BF16)],
        compiler_params=_params("arbitrary", "arbitrary"),
        name="conv_in",
    )(h, w_in, w_in, w_in)


def _halo_flags(i, nt):
    has_prev = (i >= 2).astype(F32)
    has_next = jnp.logical_and(i >= 1, i < nt - 1).astype(F32)
    return has_prev, has_next


def _conv_mix_kernel(p_ref, pp_ref, pn_ref, gb_ref, cw_ref, r_ref, ext):
    has_prev, has_next = _halo_flags(pl.program_id(0), NT_ALL)
    ext[0:HALO] = pp_ref[...] * has_prev
    ext[HALO:HALO + TM] = p_ref[...]
    ext[HALO + TM:] = pn_ref[...] * has_next
    q = (ext[HALO - 1:HALO - 1 + TM] * cw_ref[0:1]
         + ext[HALO:HALO + TM] * cw_ref[1:2]
         + ext[HALO + 1:HALO + 1 + TM] * cw_ref[2:3])
    r_ref[...] = (gb_ref[...] * q).astype(r_ref.dtype)


def _halo_specs(width):
    per = TM // HALO
    last = T_ALL // HALO - 1
    prev = pl.BlockSpec((HALO, width), lambda i: (jnp.maximum(i * per - 1, 0), 0))
    nxt = pl.BlockSpec((HALO, width), lambda i: (jnp.minimum((i + 1) * per, last), 0))
    return prev, nxt


def _conv_mix(p, gb, conv_w):
    prev, nxt = _halo_specs(D_MODEL)
    return pl.pallas_call(
        _conv_mix_kernel,
        grid=(NT_ALL,),
        in_specs=[_row_spec(D_MODEL), prev, nxt, _row_spec(D_MODEL), _full_spec((3, D_MODEL))],
        out_specs=_row_spec(D_MODEL),
        out_shape=jax.ShapeDtypeStruct((T_ALL, D_MODEL), BF16),
        scratch_shapes=[pltpu.VMEM((TM + 2 * HALO, D_MODEL), F32)],
        compiler_params=_params("arbitrary"),
        name="conv_mix",
    )(p, p, p, gb, conv_w)


def _mixer_out_kernel(*refs):
    r_ref, w_ref, *x_refs, gate_ref, g_ref, b_ref, o_ref, w_bf16 = refs
    _cast_once(w_ref, w_bf16, pl.program_id(0) == 0)
    o = _dot(r_ref[...], w_bf16[...])
    y = ALPHA * _stream_tile(x_refs) + gate_ref[0] * o
    o_ref[...] = _layer_norm(y, g_ref[...], b_ref[...])


def _mixer_out(r, w_out, x, gate, ln_g, ln_b, *, kind0):
    nt = r.shape[0] // TM
    x_specs, x_args = _stream_specs(x, NT_ALL - nt)
    w_spec = pl.BlockSpec((D_MODEL, D_MODEL), lambda i: (0, 0), pipeline_mode=pl.Buffered(1))
    return pl.pallas_call(
        _mixer_out_kernel,
        grid=(nt,),
        in_specs=[_row_spec(D_MODEL), w_spec] + x_specs
                 + [_mod_spec(kind0), _full_spec((1, D_MODEL)), _full_spec((1, D_MODEL))],
        out_specs=_row_spec(D_MODEL),
        out_shape=jax.ShapeDtypeStruct((nt * TM, D_MODEL), F32),
        scratch_shapes=[pltpu.VMEM((D_MODEL, D_MODEL), BF16)],
        compiler_params=_params("arbitrary"),
        name="mixer_out",
    )(r, w_out, *x_args, gate, ln_g, ln_b)


def _pool_kernel(x_ref, xp_ref, xn_ref, sh_ref, sc_ref, gate_ref, w_ref, ps_ref, g_ref, b_ref, o_ref, ext):
    i = pl.program_id(0)
    has_prev, has_next = _halo_flags(i, NT_ALL)
    one_sc = 1.0 + sc_ref[0]
    sh = sh_ref[0]
    x = x_ref[...]
    h = x * one_sc + sh
    ext[0:HALO] = (xp_ref[...] * one_sc + sh) * has_prev
    ext[HALO:HALO + TM] = h
    ext[HALO + TM:] = (xn_ref[...] * one_sc + sh) * has_next
    t_loc = lax.broadcasted_iota(jnp.int32, (TM, 1), 0) + jnp.maximum(i - 1, 0) * TM
    n_seq = jnp.where(i == 0, CTX_LEN, SEQ)
    mixed = []
    for g, win in enumerate(POOL_WINDOWS):
        c0 = g * POOL_GROUP
        half = win // 2
        s = ext[HALO - half:HALO - half + TM, c0:c0 + POOL_GROUP]
        for j in range(1 - half, win - half):
            s = s + ext[HALO + j:HALO + j + TM, c0:c0 + POOL_GROUP]
        lo = jnp.maximum(t_loc - half, 0)
        hi = jnp.minimum(t_loc + (win - half), n_seq)
        cnt = (hi - lo).astype(F32)
        pooled = s / cnt - h[:, c0:c0 + POOL_GROUP]
        mixed.append(_dot(pooled.astype(BF16), w_ref[g]))
    o = jnp.concatenate(mixed, axis=1) * ps_ref[...]
    y = ALPHA * x + gate_ref[0] * o
    o_ref[...] = _layer_norm(y, g_ref[...], b_ref[...])


def _pool_layer(x, shift, scale, gate, w_grp, pool_scale, ln_g, ln_b):
    prev, nxt = _halo_specs(D_MODEL)
    return pl.pallas_call(
        _pool_kernel,
        grid=(NT_ALL,),
        in_specs=[_row_spec(D_MODEL), prev, nxt, _mod_spec(0), _mod_spec(0), _mod_spec(0),
                  _full_spec((len(POOL_WINDOWS), POOL_GROUP, POOL_GROUP)), _full_spec((1, D_MODEL)),
                  _full_spec((1, D_MODEL)), _full_spec((1, D_MODEL))],
        out_specs=_row_spec(D_MODEL),
        out_shape=jax.ShapeDtypeStruct((T_ALL, D_MODEL), F32),
        scratch_shapes=[pltpu.VMEM((TM + 2 * HALO, D_MODEL), F32)],
        compiler_params=_params("arbitrary"),
        name="pool_layer",
    )(x, x, x, shift, scale, gate, w_grp, pool_scale, ln_g, ln_b)


PROJ_TN = 512


def _proj_kernel(*refs, norm, rope, scale, chunk_width, group_heads):
    refs = list(refs)
    h_ref, w_ref = refs[:2]
    rest = refs[2:]
    gain_ref = rest.pop(0) if norm else None
    if rope:
        c_ref, a_ref, b_ref = rest[:3]
        rest = rest[3:]
    o_ref, w_bf16 = rest
    _cast_once(w_ref, w_bf16, pl.program_id(1) == 0)

    def tile_dot(t):
        return _dot(h_ref[t * TM:(t + 1) * TM, :], w_bf16[...])
    acc_next = tile_dot(0)
    for t in range(PROJ_ROW_TILES):
        acc = acc_next
        if t + 1 < PROJ_ROW_TILES:
            acc_next = tile_dot(t + 1)
        rows = slice(t * TM, (t + 1) * TM)
        if chunk_width:
            for hh in range(PROJ_TN // chunk_width):
                o_ref[hh, t] = acc[:, hh * chunk_width:(hh + 1) * chunk_width].T.astype(o_ref.dtype)
            continue
        for hh in range(PROJ_TN // HEAD_DIM):
            xh = acc[:, hh * HEAD_DIM:(hh + 1) * HEAD_DIM]
            if norm:
                ms = jnp.mean(xh * xh, axis=-1, keepdims=True)
                xh = xh * lax.rsqrt(ms + RMS_EPS) * gain_ref[...]
            if rope:
                xh = (xh * c_ref[rows, :] + pltpu.roll(xh, HEAD_DIM - 32, axis=1) * a_ref[rows, :]
                      + pltpu.roll(xh, 32, axis=1) * b_ref[rows, :])
            if scale != 1.0:
                xt = (xh * scale).T.astype(o_ref.dtype)
                if group_heads:
                    o_ref[0, t, :, hh * TM:(hh + 1) * TM] = xt
                else:
                    o_ref[hh * HEAD_DIM:(hh + 1) * HEAD_DIM, rows] = xt
            else:
                o_ref[rows, hh * HEAD_DIM:(hh + 1) * HEAD_DIM] = xh.astype(o_ref.dtype)


PROJ_ROW_TILES = 3


def _project(h, w, col0, width, *, gain=None, rope_tabs=None, scale=1.0, chunk_width=0, group_heads=False):
    assert NT_ALL % PROJ_ROW_TILES == 0
    rt = PROJ_ROW_TILES
    nb = width // PROJ_TN
    off = col0 // PROJ_TN
    in_specs = [pl.BlockSpec((rt * TM, D_MODEL), lambda j, i: (i, 0)),
                pl.BlockSpec((D_MODEL, PROJ_TN), lambda j, i: (0, j + off))]
    args = [h, w]
    if gain is not None:
        in_specs.append(pl.BlockSpec((1, HEAD_DIM), lambda j, i: (0, 0)))
        args.append(gain)
    if rope_tabs is not None:
        in_specs += [pl.BlockSpec((rt * TM, HEAD_DIM), lambda j, i: (i, 0))] * 3
        args += list(rope_tabs)
    if chunk_width:
        per = PROJ_TN // chunk_width
        out_spec = pl.BlockSpec((per, rt, chunk_width, TM), lambda j, i: (j, i, 0, 0))
        out_shape = jax.ShapeDtypeStruct((width // chunk_width, NT_ALL, chunk_width, TM), BF16)
    elif scale != 1.0 and group_heads:
        per = PROJ_TN // HEAD_DIM
        out_spec = pl.BlockSpec((1, rt, HEAD_DIM, per * TM), lambda j, i: (j, i, 0, 0))
        out_shape = jax.ShapeDtypeStruct((nb, NT_ALL, HEAD_DIM, per * TM), BF16)
    elif scale != 1.0:
        out_spec = pl.BlockSpec((PROJ_TN, rt * TM), lambda j, i: (j, i))
        out_shape = jax.ShapeDtypeStruct((width, T_ALL), BF16)
    else:
        out_spec = pl.BlockSpec((rt * TM, PROJ_TN), lambda j, i: (i, j))
        out_shape = jax.ShapeDtypeStruct((T_ALL, width), BF16)
    kern = functools.partial(_proj_kernel, norm=gain is not None, rope=rope_tabs is not None, scale=scale,
                             chunk_width=chunk_width, group_heads=group_heads)
    return pl.pallas_call(
        kern,
        grid=(nb, NT_ALL // rt),
        in_specs=in_specs,
        out_specs=out_spec,
        out_shape=out_shape,
        scratch_shapes=[pltpu.VMEM((D_MODEL, PROJ_TN), BF16)],
        compiler_params=_params("arbitrary", "arbitrary"),
        name="project",
    )(*args)


def _softmax_update(s, vt_c, m_ref, l_ref, acc_ref):
    m_prev = m_ref[...]
    m_new = jnp.maximum(m_prev, jnp.max(s, axis=0, keepdims=True))
    p = jnp.exp2(s - m_new)
    alpha = jnp.exp2(m_prev - m_new)
    l_ref[...] = alpha * l_ref[...] + jnp.sum(p, axis=0, keepdims=True)
    acc_ref[...] = alpha * acc_ref[...] + _dot(vt_c, p.astype(BF16))
    m_ref[...] = m_new


def _key_tile(k_ref, j):
    return k_ref[j * TM:(j + 1) * TM, :]


def _softmax_pipeline(steps, m_ref, l_ref, acc_ref):
    s_next = steps[0][0]()
    for i, (_, vt_c, stream) in enumerate(steps):
        s_cur = s_next
        if i + 1 < len(steps):
            s_next = steps[i + 1][0]()
        _softmax_update(s_cur, vt_c, m_ref.at[stream], l_ref.at[stream], acc_ref.at[stream])


def _attend(make_steps, context_queries, m_ref, l_ref, acc_ref):
    m_ref[...] = jnp.full(m_ref.shape, -jnp.inf, F32)
    l_ref[...] = jnp.zeros(l_ref.shape, F32)
    acc_ref[...] = jnp.zeros(acc_ref.shape, F32)

    if context_queries is not None:
        @pl.when(context_queries)
        def _():
            _softmax_pipeline(make_steps(0), m_ref, l_ref, acc_ref)

    def all_tiles():
        steps = []
        for j in range(NT_ALL):
            steps += make_steps(j)
        _softmax_pipeline(steps, m_ref, l_ref, acc_ref)

    if context_queries is None:
        all_tiles()
    else:
        pl.when(jnp.logical_not(context_queries))(all_tiles)


def _gqa_kernel(qt_ref, k_ref, vt_ref, o_ref, m_ref, l_ref, acc_ref):
    wide = GQA_STREAM_HEADS * TM

    def make_steps(j):
        k_c = _key_tile(k_ref, j)
        vt_c = vt_ref[0, j]
        return [(functools.partial(lambda s: _dot(k_c, qt_ref[0, 0, :, s * wide:(s + 1) * wide]), s), vt_c, s)
                for s in range(GQA_GROUP // GQA_STREAM_HEADS)]
    _attend(make_steps, pl.program_id(1) == 0, m_ref, l_ref, acc_ref)

    for g in range(GQA_GROUP):
        s, c = divmod(g, GQA_STREAM_HEADS)
        o = acc_ref[s, :, c * TM:(c + 1) * TM] / l_ref[s, :, c * TM:(c + 1) * TM]
        o_ref[:, g * HEAD_DIM:(g + 1) * HEAD_DIM] = o.T.astype(o_ref.dtype)


GQA_STREAM_HEADS = 2


def _gqa_attention(qt, k, vt):
    wide = GQA_GROUP * TM
    streams = GQA_GROUP // GQA_STREAM_HEADS
    sw = GQA_STREAM_HEADS * TM
    return pl.pallas_call(
        _gqa_kernel,
        grid=(GQA_KV_HEADS, NT_ALL),
        in_specs=[pl.BlockSpec((1, 1, HEAD_DIM, wide), lambda kv, i: (kv, i, 0, 0)),
                  pl.BlockSpec((T_ALL, HEAD_DIM), lambda kv, i: (0, kv)),
                  pl.BlockSpec((1, NT_ALL, HEAD_DIM, TM), lambda kv, i: (kv, 0, 0, 0))],
        out_specs=pl.BlockSpec((TM, GQA_GROUP * HEAD_DIM), lambda kv, i: (i, kv)),
        out_shape=jax.ShapeDtypeStruct((T_ALL, D_MODEL), BF16),
        scratch_shapes=[pltpu.VMEM((streams, 1, sw), F32), pltpu.VMEM((streams, 1, sw), F32),
                        pltpu.VMEM((streams, HEAD_DIM, sw), F32)],
        compiler_params=_params("arbitrary", "arbitrary"),
        name="gqa_attention",
    )(qt, k, vt)


def _diff_kernel(qt_ref, k_ref, vt_ref, lam_ref, sub_ref, o_ref, m_ref, l_ref, acc_ref, *, lam_init):
    def make_steps(j):
        k_c = _key_tile(k_ref, j)
        vt_c = vt_ref[0, j]
        return [(functools.partial(lambda s: _dot(k_c[:, s * HEAD_DIM:(s + 1) * HEAD_DIM],
                                                  qt_ref[s * HEAD_DIM:(s + 1) * HEAD_DIM, :]), s), vt_c, s)
                for s in range(2)]
    _attend(make_steps, None, m_ref, l_ref, acc_ref)

    lv = lam_ref[...]
    lam = (jnp.exp(jnp.sum(lv[0:1] * lv[1:2], axis=-1, keepdims=True))
           - jnp.exp(jnp.sum(lv[2:3] * lv[3:4], axis=-1, keepdims=True)) + lam_init)
    o = (acc_ref[0] / l_ref[0] - lam * (acc_ref[1] / l_ref[1])).T
    ms = jnp.mean(o * o, axis=-1, keepdims=True)
    o = o * lax.rsqrt(ms + RMS_EPS) * sub_ref[...] * (1.0 - lam_init)
    o_ref[...] = o.astype(o_ref.dtype)


def _diff_attention(qt, k, vt, lam_vecs, subln, lam_init):
    wv = 2 * HEAD_DIM
    return pl.pallas_call(
        functools.partial(_diff_kernel, lam_init=lam_init),
        grid=(DIFF_HEADS, SEQ // TM),
        in_specs=[pl.BlockSpec((wv, TM), lambda hd, i: (hd, i + 1)),
                  pl.BlockSpec((T_ALL, wv), lambda hd, i: (0, hd)),
                  pl.BlockSpec((1, NT_ALL, wv, TM), lambda hd, i: (hd, 0, 0, 0)),
                  pl.BlockSpec((4, HEAD_DIM), lambda hd, i: (0, 0)), pl.BlockSpec((1, wv), lambda hd, i: (0, 0))],
        out_specs=pl.BlockSpec((TM, wv), lambda hd, i: (i, hd)),
        out_shape=jax.ShapeDtypeStruct((SEQ, D_MODEL), BF16),
        scratch_shapes=[pltpu.VMEM((2, 1, TM), F32), pltpu.VMEM((2, 1, TM), F32), pltpu.VMEM((2, wv, TM), F32)],
        compiler_params=_params("arbitrary", "arbitrary"),
        name="diff_attention",
    )(qt, k, vt, lam_vecs, subln)


def _lane_pick(vals, lane, idx):
    return jnp.sum(jnp.where(lane == idx, vals, 0.0), axis=-1, keepdims=True)


def _router_kernel(x_ref, sh_ref, sc_ref, w_ref, b_ref, meta_ref, cnt_ref, carry):
    i = pl.program_id(0)

    @pl.when(i == 0)
    def _():
        carry[...] = jnp.zeros(carry.shape, F32)

    h = x_ref[...] * (1.0 + sc_ref[0]) + sh_ref[0]
    h_hi = h.astype(BF16)
    h_lo = (h - h_hi.astype(F32)).astype(BF16)
    hw = _dot(h_hi, w_ref[...])
    logits = hw[:, 0:LANES] + hw[:, LANES:2 * LANES] + _dot(h_lo, w_ref[:, 0:LANES]) + b_ref[...]
    lane = lax.broadcasted_iota(jnp.int32, (TM, LANES), 1).astype(F32)
    neg = -jnp.inf
    big = float(LANES)

    grp = jnp.where(lane < N_GROUPS, logits, neg)
    gmax = jnp.max(grp, axis=-1, keepdims=True)
    gstar = jnp.min(jnp.where(grp == gmax, lane, big), axis=-1, keepdims=True)
    p_group = 1.0 / jnp.sum(jnp.exp(grp - gmax), axis=-1, keepdims=True)

    e_lo = N_GROUPS + EXPERTS_PER_GROUP * gstar
    own = jnp.logical_and(lane >= e_lo, lane < e_lo + EXPERTS_PER_GROUP)
    le = jnp.where(own, logits, neg)
    v1 = jnp.max(le, axis=-1, keepdims=True)
    i1 = jnp.min(jnp.where(le == v1, lane, big), axis=-1, keepdims=True)
    le2 = jnp.where(lane == i1, neg, le)
    v2 = jnp.max(le2, axis=-1, keepdims=True)
    i2 = jnp.min(jnp.where(le2 == v2, lane, big), axis=-1, keepdims=True)
    t2 = jnp.exp(v2 - v1)
    w1 = p_group / (1.0 + t2)
    w2 = p_group * t2 / (1.0 + t2)
    e1 = i1 - N_GROUPS
    e2 = i2 - N_GROUPS

    chosen = jnp.logical_or(lane == e1, lane == e2)
    onehot = jnp.where(chosen, 1.0, 0.0)
    r_io = lax.broadcasted_iota(jnp.int32, (TM, TM), 0)
    c_io = lax.broadcasted_iota(jnp.int32, (TM, TM), 1)
    tri = jnp.where(c_io < r_io, 1.0, 0.0).astype(BF16)
    before = carry[...] + _dot(tri, onehot.astype(BF16))
    r1 = _lane_pick(before, lane, e1)
    r2 = _lane_pick(before, lane, e2)
    carry[...] = carry[...] + jnp.sum(onehot, axis=0, keepdims=True)

    meta = jnp.where(lane == 0, e1, 0.0)
    for col, val in ((1, e2), (2, r1), (3, r2), (4, w1), (5, w2)):
        meta = jnp.where(lane == col, val, meta)
    meta_ref[...] = meta
    cnt_ref[...] = jnp.broadcast_to(carry[...], cnt_ref.shape)


def _router(x, shift, scale, w_r, b_r, *, kind0):
    nt = x.shape[0] // TM
    return pl.pallas_call(
        _router_kernel,
        grid=(nt,),
        in_specs=[_row_spec(D_MODEL), _mod_spec(kind0), _mod_spec(kind0),
                  _full_spec((D_MODEL, 2 * LANES)), _full_spec((1, LANES))],
        out_specs=[_row_spec(LANES), _full_spec((SUBLANES, LANES))],
        out_shape=[jax.ShapeDtypeStruct((nt * TM, LANES), F32), jax.ShapeDtypeStruct((SUBLANES, LANES), F32)],
        scratch_shapes=[pltpu.VMEM((1, LANES), F32)],
        compiler_params=_params("arbitrary"),
        name="router",
    )(x, shift, scale, w_r, b_r)


def _row_copy(src, src_row, dst, dst_row, sem):
    return pltpu.make_async_copy(src.at[pl.ds(src_row, 1)], dst.at[pl.ds(dst_row, 1)], sem)


def _dispatch_kernel(pos_ref, ztile_ref, x_ref, sh_ref, sc_ref, out_ref, hbuf, zbuf, sem, zsem, *, n_tok):
    i = pl.program_id(0)

    @pl.when(i == 0)
    def _():
        zbuf[...] = jnp.zeros(zbuf.shape, F32)
        for phase in ("start", "wait"):
            for j in range(ztile_ref.shape[0]):
                tile = ztile_ref[j]

                @pl.when(tile >= 0)
                def _():
                    dst = out_ref.at[pl.ds(pl.multiple_of(tile * TM, TM), TM)]
                    getattr(pltpu.make_async_copy(zbuf, dst, zsem), phase)()

    hbuf[...] = x_ref[...] * (1.0 + sc_ref[0]) + sh_ref[0]

    def start(r, carry):
        for k in range(2):
            _row_copy(hbuf, r, out_ref, pos_ref[k * n_tok + i * TM + r], sem).start()
        return carry
    lax.fori_loop(0, TM, start, 0, unroll=ROW_DMA_UNROLL)
    for k in range(2):
        _tile_wait(hbuf, out_ref, sem)


ROW_DMA_UNROLL = 8


def _tile_wait(vmem_tile, hbm, sem):
    pltpu.make_async_copy(hbm.at[pl.ds(0, TM)], vmem_tile, sem).wait()


def _dispatch(pos, zero_tiles, x, shift, scale, n_sorted, *, kind0):
    nt = x.shape[0] // TM
    return pl.pallas_call(
        functools.partial(_dispatch_kernel, n_tok=x.shape[0]),
        grid_spec=pltpu.PrefetchScalarGridSpec(
            num_scalar_prefetch=2,
            grid=(nt,),
            in_specs=[_row_spec(D_MODEL), _mod_spec(kind0), _mod_spec(kind0)],
            out_specs=pl.BlockSpec(memory_space=pl.ANY),
            scratch_shapes=[pltpu.VMEM((TM, D_MODEL), F32), pltpu.VMEM((TM, D_MODEL), F32),
                            pltpu.SemaphoreType.DMA, pltpu.SemaphoreType.DMA],
        ),
        out_shape=jax.ShapeDtypeStruct((n_sorted, D_MODEL), F32),
        compiler_params=_params("arbitrary"),
        name="moe_dispatch",
    )(pos, zero_tiles, x, shift, scale)


PLAN_EXPERT = 0
PLAN_FIRST = 1
PLAN_SLOT = 2
PLAN_NEXT = 3
PLAN_ROWS_VALID = 4
PLAN_SRC_TILE = 5
PLAN_ROWS = 6


def _ffn_kernel(plan_ref, a_ref, wg_hbm, wu_hbm, wd_hbm, o_ref, wg_buf, wu_buf, wd_buf, sems):
    i = pl.program_id(0)
    slot = plan_ref[PLAN_SLOT, i]
    rows = plan_ref[PLAN_ROWS_VALID, i]

    def weight_copies(expert, s):
        return [pltpu.make_async_copy(hbm.at[expert], buf.at[s], sems.at[s, j])
                for j, (hbm, buf) in enumerate(((wg_hbm, wg_buf), (wu_hbm, wu_buf), (wd_hbm, wd_buf)))]

    @pl.when(i == 0)
    def _():
        for c in weight_copies(plan_ref[PLAN_EXPERT, 0], 0):
            c.start()

    @pl.when(plan_ref[PLAN_FIRST, i] == 1)
    def _():
        for c in weight_copies(plan_ref[PLAN_EXPERT, i], slot):
            c.wait()
        nxt = plan_ref[PLAN_NEXT, i]

        @pl.when(nxt >= 0)
        def _():
            for c in weight_copies(nxt, 1 - slot):
                c.start()

    @pl.when(rows > 0)
    def _():
        a = a_ref[...].astype(BF16)
        g = _dot(a, wg_buf[slot].astype(BF16))
        u = _dot(a, wu_buf[slot].astype(BF16))
        act = (g * jax.nn.sigmoid(g) * u).astype(BF16)
        o_ref[...] = _dot(act, wd_buf[slot].astype(BF16))

    @pl.when(rows == 0)
    def _():
        o_ref[...] = jnp.zeros(o_ref.shape, F32)


def _expert_ffn(plan, a_sorted, w_gate, w_up, w_down):
    n_tiles = a_sorted.shape[0] // TM
    return pl.pallas_call(
        _ffn_kernel,
        grid_spec=pltpu.PrefetchScalarGridSpec(
            num_scalar_prefetch=1,
            grid=(n_tiles,),
            in_specs=[pl.BlockSpec((TM, D_MODEL), lambda i, plan: (plan[PLAN_SRC_TILE, i], 0)),
                      pl.BlockSpec(memory_space=pl.ANY), pl.BlockSpec(memory_space=pl.ANY),
                      pl.BlockSpec(memory_space=pl.ANY)],
            out_specs=pl.BlockSpec((TM, D_MODEL), lambda i, plan: (i, 0)),
            scratch_shapes=[pltpu.VMEM((2, D_MODEL, EXPERT_FF), F32), pltpu.VMEM((2, D_MODEL, EXPERT_FF), F32),
                            pltpu.VMEM((2, EXPERT_FF, D_MODEL), F32), pltpu.SemaphoreType.DMA((2, 3))],
        ),
        out_shape=jax.ShapeDtypeStruct(a_sorted.shape, F32),
        compiler_params=_params("arbitrary"),
        name="moe_experts",
    )(plan, a_sorted, w_gate, w_up, w_down)


def _combine_kernel(pos_ref, y_ref, x_ref, meta_ref, gate_ref, g_ref, b_ref, *rest, emit_h, nt):
    if emit_h:
        nsh_ref, nsc_ref, o_ref, h_ref, bufs, sems = rest
    else:
        o_ref, bufs, sems = rest
    i = pl.program_id(0)

    def gather(tile, slot):
        def start(r, carry):
            for k in range(2):
                _row_copy(y_ref, pos_ref[k * (nt * TM) + tile * TM + r], bufs.at[slot, k], r, sems.at[slot]).start()
            return carry
        lax.fori_loop(0, TM, start, 0, unroll=ROW_DMA_UNROLL)

    @pl.when(i == 0)
    def _():
        gather(0, 0)

    @pl.when(i + 1 < nt)
    def _():
        gather(i + 1, (i + 1) & 1)

    slot = i & 1
    for k in range(2):
        _tile_wait(bufs.at[slot, k], y_ref, sems.at[slot])

    meta = meta_ref[...]
    moe = meta[:, 4:5] * bufs[slot, 0] + meta[:, 5:6] * bufs[slot, 1]
    y = ALPHA * x_ref[...] + gate_ref[0] * moe
    out = _layer_norm(y, g_ref[...], b_ref[...])
    o_ref[...] = out
    if emit_h:
        h_ref[...] = (out * (1.0 + nsc_ref[0]) + nsh_ref[0]).astype(h_ref.dtype)


def _combine(pos, y_sorted, x, meta, gate, ln_g, ln_b, next_mod, *, kind0):
    nt = x.shape[0] // TM
    emit_h = next_mod is not None
    mspec = _mod_spec(kind0)
    vec = _full_spec((1, D_MODEL))
    ospec = _row_spec(D_MODEL)
    in_specs = [pl.BlockSpec(memory_space=pl.ANY), ospec, _row_spec(LANES), mspec, vec, vec]
    args = [pos, y_sorted, x, meta, gate, ln_g, ln_b]
    out_specs = [ospec]
    out_shape = [jax.ShapeDtypeStruct((nt * TM, D_MODEL), F32)]
    if emit_h:
        in_specs += [mspec, mspec]
        args += list(next_mod)
        out_specs.append(ospec)
        out_shape.append(jax.ShapeDtypeStruct((nt * TM, D_MODEL), BF16))
    res = pl.pallas_call(
        functools.partial(_combine_kernel, emit_h=emit_h, nt=nt),
        grid_spec=pltpu.PrefetchScalarGridSpec(
            num_scalar_prefetch=1,
            grid=(nt,),
            in_specs=in_specs,
            out_specs=out_specs,
            scratch_shapes=[pltpu.VMEM((2, 2, TM, D_MODEL), F32), pltpu.SemaphoreType.DMA((2,))],
        ),
        out_shape=out_shape,
        compiler_params=_params("arbitrary"),
        name="moe_combine",
    )(*args)
    return res if emit_h else (res[0], None)


def _moe_layer(x, mods, w_r, b_r, w_gate, w_up, w_down, ln_g, ln_b, next_mod, *, kind0, expert0):
    shift, scale, gate = mods
    n_tok = x.shape[0]
    n_tiles = -(-(2 * n_tok + N_EXPERTS * (TM - 1)) // TM)
    meta, cnt = _router(x, shift, scale, w_r, b_r, kind0=kind0)

    i32 = jnp.int32
    counts = cnt[0, :N_EXPERTS].astype(i32)
    padded = ((counts + TM - 1) // TM) * TM
    ends = jnp.cumsum(padded)
    offs = ends - padded
    meta_t = meta[:, :SUBLANES].T
    experts, ranks = meta_t[0:2].astype(i32), meta_t[2:4].astype(i32)
    ids = jnp.arange(N_EXPERTS, dtype=i32)
    own_offs = jnp.sum(jnp.where(experts[..., None] == ids, offs, 0), axis=-1)
    pos = (own_offs + ranks).reshape(-1)

    n_valid = ends[-1] // TM
    tiles = jnp.arange(n_tiles, dtype=i32)
    tile_start = tiles * TM
    tile_valid = tile_start < ends[-1]
    te = jnp.sum(tile_start[:, None] >= ends[None, :], axis=1).astype(i32)
    te_last = jnp.sum((n_valid - 1) * TM >= ends).astype(i32)
    te = jnp.where(tile_valid, te, te_last)
    nonempty = counts > 0
    run = jnp.cumsum(nonempty.astype(i32)) - 1
    later = jnp.where(nonempty[None, :] & (ids[None, :] > ids[:, None]), ids[None, :], N_EXPERTS)
    nxt = jnp.min(later, axis=1)
    nxt = jnp.where(nxt < N_EXPERTS, nxt + expert0, -1)
    plan = jnp.stack([
        te + expert0,
        (tile_valid & (tile_start == offs[te])).astype(i32),
        run[te] & 1,
        nxt[te],
        jnp.where(tile_valid, jnp.clip(counts[te] - (tile_start - offs[te]), 0, TM), 0),
        jnp.minimum(tiles, n_valid - 1),
    ]).astype(i32)
    assert plan.shape[0] == PLAN_ROWS

    partial = jnp.where((counts % TM) != 0, ends // TM - 1, -1)
    tail = n_valid + jnp.arange(n_tiles - 2 * (n_tok // TM), dtype=i32)
    zero_tiles = jnp.concatenate([partial, jnp.where(tail < n_tiles, tail, -1)]).astype(i32)

    a_sorted = _dispatch(pos, zero_tiles, x, shift, scale, n_tiles * TM, kind0=kind0)
    y_sorted = _expert_ffn(plan, a_sorted, w_gate, w_up, w_down)
    return _combine(pos, y_sorted, x, meta, gate, ln_g, ln_b, next_mod, kind0=kind0)


def _rope_tables():
    rows = SEQ // GRID_W
    axis_dim = HEAD_DIM // 2
    inv = ROPE_THETA ** (-jnp.arange(0, axis_dim, 2, dtype=F32) / axis_dim)
    row_ang = jnp.arange(rows, dtype=F32)[:, None] * inv
    col_ang = jnp.arange(GRID_W, dtype=F32)[:, None] * inv
    cr, sr = (jnp.repeat(f(row_ang), GRID_W, axis=0) for f in (jnp.cos, jnp.sin))
    cc, sc = (jnp.tile(f(col_ang), (rows, 1)) for f in (jnp.cos, jnp.sin))
    z = jnp.zeros_like(sr)
    c_tab = jnp.concatenate([cr, cr, cc, cc], axis=1)
    a_tab = jnp.concatenate([-sr, z, -sc, z], axis=1)
    b_tab = jnp.concatenate([z, sr, z, sc], axis=1)
    ctx = (jnp.ones((CTX_LEN, HEAD_DIM), F32), jnp.zeros((CTX_LEN, HEAD_DIM), F32), jnp.zeros((CTX_LEN, HEAD_DIM), F32))
    return tuple(jnp.concatenate([c0, t], axis=0) for c0, t in zip(ctx, (c_tab, a_tab, b_tab)))


def kernel(x, c, ctx, c_ctx, mod_w, mod_b, ln_g, ln_b, rt_grp_w, rt_grp_b, rt_exp_w, rt_exp_b,
           ex_w_gate, ex_w_up, ex_w_down, conv_in_w, conv_w, conv_out_w, pool_w, pool_scale,
           gqa_qkv_w, gqa_qk_norm, gqa_out_w, diff_qkv_w, diff_lambda, diff_subln, diff_out_w):
    assert x.shape == (1, SEQ, D_MODEL) and ctx.shape == (1, CTX_LEN, D_MODEL)
    xs = (ctx[0], x[0])

    c_cols = jnp.pad(jnp.stack([c_ctx, c[0]], axis=1), ((0, 0), (0, LANES - 2)))
    mods = _modulation(c_cols, mod_w, mod_b)
    mods = mods[:, :2].reshape(DEPTH, 2, N_MOD, 1, D_MODEL)
    mod = lambda layer, m: mods[layer, :, m]

    n_pad = LANES - N_GROUPS - N_EXPERTS
    w_r = jnp.concatenate([rt_grp_w, rt_exp_w, jnp.zeros((DEPTH, D_MODEL, n_pad), F32)], axis=-1)
    w_hi = w_r.astype(BF16)
    w_r = jnp.concatenate([w_hi, (w_r - w_hi.astype(F32)).astype(BF16)], axis=-1)
    b_r = jnp.concatenate([rt_grp_b, rt_exp_b, jnp.zeros((DEPTH, n_pad), F32)], axis=-1)[:, None, :]
    rope = _rope_tables()
    sm_scale = HEAD_DIM ** -0.5 * math.log2(math.e)
    bf = lambda w: w.astype(BF16)

    stack = lambda w: w.reshape((DEPTH * N_EXPERTS,) + w.shape[2:])
    w_gate, w_up, w_down = stack(ex_w_gate), stack(ex_w_up), stack(ex_w_down)

    def moe(layer, x_in, next_mod, kind0=0):
        return _moe_layer(x_in, (mod(layer, 3), mod(layer, 4), mod(layer, 5)), w_r[layer], b_r[layer],
                          w_gate, w_up, w_down, ln_g[layer, 1:2], ln_b[layer, 1:2], next_mod,
                          kind0=kind0, expert0=layer * N_EXPERTS)

    h = _modulate(xs, mod(0, 0), mod(0, 1))
    gb, p = _conv_in(h, conv_in_w[0])
    r = _conv_mix(p, gb, conv_w[0])
    xs = _mixer_out(r, conv_out_w[0], xs, mod(0, 2), ln_g[0, 0:1], ln_b[0, 0:1], kind0=0)
    xs, _ = moe(0, xs, None)

    xs = _pool_layer(xs, mod(1, 0), mod(1, 1), mod(1, 2), bf(pool_w[0]), pool_scale[0:1],
                     ln_g[1, 0:1], ln_b[1, 0:1])
    xs, h = moe(1, xs, (mod(2, 0), mod(2, 1)))

    wqkv = gqa_qkv_w[0]
    nq, nkv = GQA_HEADS * HEAD_DIM, GQA_KV_HEADS * HEAD_DIM
    assert PROJ_TN == GQA_GROUP * HEAD_DIM
    qt = _project(h, wqkv, 0, nq, gain=gqa_qk_norm[0, 0:1], rope_tabs=rope, scale=sm_scale, group_heads=True)
    k = _project(h, wqkv, nq, nkv, gain=gqa_qk_norm[0, 1:2], rope_tabs=rope)
    vt = _project(h, wqkv, nq + nkv, nkv, chunk_width=HEAD_DIM)
    o = _gqa_attention(qt, k, vt)
    xs = _mixer_out(o, gqa_out_w[0], xs, mod(2, 2), ln_g[2, 0:1], ln_b[2, 0:1], kind0=0)
    xs, h = moe(2, xs, (mod(3, 0), mod(3, 1)))

    lam_init = 0.8 - 0.6 * math.exp(-0.3 * 3)
    wqkv = diff_qkv_w[0]
    qt = _project(h, wqkv, 0, D_MODEL, rope_tabs=rope, scale=sm_scale)
    k = _project(h, wqkv, D_MODEL, D_MODEL, rope_tabs=rope)
    vt = _project(h, wqkv, 2 * D_MODEL, D_MODEL, chunk_width=2 * HEAD_DIM)
    o = _diff_attention(qt, k, vt, diff_lambda[0], diff_subln[0:1], lam_init)
    xs = _mixer_out(o, diff_out_w[0], xs, mod(3, 2), ln_g[3, 0:1], ln_b[3, 0:1], kind0=1)
    out, _ = moe(3, xs, None, kind0=1)
    return out[None]
```

```python
import functools
import math

import jax
import jax.numpy as jnp
from jax import lax
from jax.experimental import pallas as pl
from jax.experimental.pallas import tpu as pltpu

D_MODEL = 2048
SEQ = 8192
DEPTH = 4
GRID_W = 64
CTX_LEN = 256
ALPHA = (2.0 * DEPTH) ** 0.25
ROPE_THETA = 10000.0
LN_EPS = 1e-6
RMS_EPS = 1e-6
N_MOD = 6
HEAD_DIM = 128
GQA_HEADS = D_MODEL // HEAD_DIM
GQA_KV_HEADS = GQA_HEADS // 4
GQA_GROUP = GQA_HEADS // GQA_KV_HEADS
DIFF_HEADS = D_MODEL // (2 * HEAD_DIM)
POOL_WINDOWS = (2, 4, 8, 16)
POOL_GROUP = D_MODEL // len(POOL_WINDOWS)
N_GROUPS = 4
EXPERTS_PER_GROUP = 4
N_EXPERTS = N_GROUPS * EXPERTS_PER_GROUP
EXPERT_FF = 3 * D_MODEL // 8

LANES = 128
SUBLANES = 8
TM = 256
T_ALL = CTX_LEN + SEQ
NT_ALL = T_ALL // TM
HALO = SUBLANES
VMEM_LIMIT = 56 * 1024 * 1024

assert CTX_LEN == TM and SEQ % TM == 0 and DEPTH == 4

F32 = jnp.float32
BF16 = jnp.bfloat16


def _params(*sem):
    return pltpu.CompilerParams(dimension_semantics=sem, vmem_limit_bytes=VMEM_LIMIT)


def _dot(a, b):
    return jnp.dot(a, b, preferred_element_type=F32)


def _layer_norm(y, g, b):
    mu = jnp.mean(y, axis=-1, keepdims=True)
    d = y - mu
    var = jnp.mean(d * d, axis=-1, keepdims=True)
    return d * lax.rsqrt(var + LN_EPS) * g + b


def _mod_spec(kind0):
    return pl.BlockSpec((1, 1, D_MODEL), lambda i, *_: (jnp.minimum(i + kind0, 1), 0, 0))


def _row_spec(width, tile0=0):
    return pl.BlockSpec((TM, width), lambda i, *_: (i + tile0, 0))


def _full_spec(shape):
    nd = len(shape)
    return pl.BlockSpec(shape, lambda i, *_: (0,) * nd)


MOD_TN = 1024


def _mod_kernel(c_ref, w_ref, b_ref, o_ref):
    w = w_ref[0]
    rows = []
    for r in range(2):
        c = c_ref[:, r:r + 1]
        s = c * jax.nn.sigmoid(c)
        rows.append(jnp.sum(w * s, axis=0, keepdims=True))
    rows.append(jnp.zeros((SUBLANES - 2, MOD_TN), F32))
    o_ref[0] = jnp.concatenate(rows, axis=0) + b_ref[0]


def _modulation(c_cols, mod_w, mod_b):
    n_out = N_MOD * D_MODEL
    return pl.pallas_call(
        _mod_kernel,
        grid=(DEPTH, n_out // MOD_TN),
        in_specs=[
            pl.BlockSpec((D_MODEL, LANES), lambda l, j: (0, 0)),
            pl.BlockSpec((1, D_MODEL, MOD_TN), lambda l, j: (l, 0, j)),
            pl.BlockSpec((1, 1, MOD_TN), lambda l, j: (l, 0, j)),
        ],
        out_specs=pl.BlockSpec((1, SUBLANES, MOD_TN), lambda l, j: (l, 0, j)),
        out_shape=jax.ShapeDtypeStruct((DEPTH, SUBLANES, n_out), F32),
        compiler_params=_params("arbitrary", "arbitrary"),
        name="modulation",
    )(c_cols, mod_w, mod_b.reshape(DEPTH, 1, n_out))


def _stream_specs(x, tile0=0):
    if isinstance(x, tuple):
        return ([pl.BlockSpec((TM, D_MODEL), lambda i, *_: (0, 0)),
                 pl.BlockSpec((TM, D_MODEL), lambda i, *_: (jnp.maximum(i - 1, 0), 0))], list(x))
    return [_row_spec(D_MODEL, tile0)], [x]


def _stream_tile(x_refs):
    if len(x_refs) == 2:
        return jnp.where(pl.program_id(0) == 0, x_refs[0][...], x_refs[1][...])
    return x_refs[0][...]


def _cast_once(w_ref, w_bf16, first_step):
    @pl.when(first_step)
    def _():
        w_bf16[...] = w_ref[...].astype(BF16)


def _modulate_kernel(*refs):
    *x_refs, sh_ref, sc_ref, o_ref = refs
    o_ref[...] = (_stream_tile(x_refs) * (1.0 + sc_ref[0]) + sh_ref[0]).astype(o_ref.dtype)


def _modulate(x, shift, scale):
    x_specs, x_args = _stream_specs(x)
    return pl.pallas_call(
        _modulate_kernel,
        grid=(NT_ALL,),
        in_specs=x_specs + [_mod_spec(0), _mod_spec(0)],
        out_specs=_row_spec(D_MODEL),
        out_shape=jax.ShapeDtypeStruct((T_ALL, D_MODEL), BF16),
        compiler_params=_params("arbitrary"),
        name="modulate",
    )(*x_args, shift, scale)


CONV_TN = 512


def _conv_in_kernel(h_ref, wb_ref, wc_ref, wv_ref, gb_ref, p_ref, w_bf16):
    for s, w_ref in enumerate((wb_ref, wc_ref, wv_ref)):
        _cast_once(w_ref, w_bf16.at[s], pl.program_id(1) == 0)
    h = h_ref[...]
    gb_ref[...] = _dot(h, w_bf16[0])
    p_ref[...] = _dot(h, w_bf16[1]) * _dot(h, w_bf16[2])


CONV_ROW_TILES = 3


def _conv_in(h, w_in):
    assert NT_ALL % CONV_ROW_TILES == 0
    rows = CONV_ROW_TILES * TM
    nb = D_MODEL // CONV_TN
    wspec = lambda off: pl.BlockSpec((D_MODEL, CONV_TN), lambda j, i: (0, j + off * nb))
    ospec = pl.BlockSpec((rows, CONV_TN), lambda j, i: (i, j))
    return pl.pallas_call(
        _conv_in_kernel,
        grid=(nb, T_ALL // rows),
        in_specs=[pl.BlockSpec((rows, D_MODEL), lambda j, i: (i, 0)), wspec(0), wspec(1), wspec(2)],
        out_specs=[ospec, ospec],
        out_shape=[jax.ShapeDtypeStruct((T_ALL, D_MODEL), F32)] * 2,
        scratch_shapes=[pltpu.VMEM((3, D_MODEL, CONV_TN), BF16)],
        compiler_params=_params("arbitrary", "arbitrary"),
        name="conv_in",
    )(h, w_in, w_in, w_in)


def _halo_flags(i, nt):
    has_prev = (i >= 2).astype(F32)
    has_next = jnp.logical_and(i >= 1, i < nt - 1).astype(F32)
    return has_prev, has_next


def _conv_mix_kernel(p_ref, pp_ref, pn_ref, gb_ref, cw_ref, r_ref, ext):
    has_prev, has_next = _halo_flags(pl.program_id(0), NT_ALL)
    ext[0:HALO] = pp_ref[...] * has_prev
    ext[HALO:HALO + TM] = p_ref[...]
    ext[HALO + TM:] = pn_ref[...] * has_next
    q = (ext[HALO - 1:HALO - 1 + TM] * cw_ref[0:1]
         + ext[HALO:HALO + TM] * cw_ref[1:2]
         + ext[HALO + 1:HALO + 1 + TM] * cw_ref[2:3])
    r_ref[...] = (gb_ref[...] * q).astype(r_ref.dtype)


def _halo_specs(width):
    per = TM // HALO
    last = T_ALL // HALO - 1
    prev = pl.BlockSpec((HALO, width), lambda i: (jnp.maximum(i * per - 1, 0), 0))
    nxt = pl.BlockSpec((HALO, width), lambda i: (jnp.minimum((i + 1) * per, last), 0))
    return prev, nxt


def _conv_mix(p, gb, conv_w):
    prev, nxt = _halo_specs(D_MODEL)
    return pl.pallas_call(
        _conv_mix_kernel,
        grid=(NT_ALL,),
        in_specs=[_row_spec(D_MODEL), prev, nxt, _row_spec(D_MODEL), _full_spec((3, D_MODEL))],
        out_specs=_row_spec(D_MODEL),
        out_shape=jax.ShapeDtypeStruct((T_ALL, D_MODEL), BF16),
        scratch_shapes=[pltpu.VMEM((TM + 2 * HALO, D_MODEL), F32)],
        compiler_params=_params("arbitrary"),
        name="conv_mix",
    )(p, p, p, gb, conv_w)


def _mixer_out_kernel(*refs):
    r_ref, w_ref, *x_refs, gate_ref, g_ref, b_ref, o_ref, w_bf16 = refs
    _cast_once(w_ref, w_bf16, pl.program_id(0) == 0)
    o = _dot(r_ref[...], w_bf16[...])
    y = ALPHA * _stream_tile(x_refs) + gate_ref[0] * o
    o_ref[...] = _layer_norm(y, g_ref[...], b_ref[...])


def _mixer_out(r, w_out, x, gate, ln_g, ln_b, *, kind0):
    nt = r.shape[0] // TM
    x_specs, x_args = _stream_specs(x, NT_ALL - nt)
    w_spec = pl.BlockSpec((D_MODEL, D_MODEL), lambda i: (0, 0), pipeline_mode=pl.Buffered(1))
    return pl.pallas_call(
        _mixer_out_kernel,
        grid=(nt,),
        in_specs=[_row_spec(D_MODEL), w_spec] + x_specs
                 + [_mod_spec(kind0), _full_spec((1, D_MODEL)), _full_spec((1, D_MODEL))],
        out_specs=_row_spec(D_MODEL),
        out_shape=jax.ShapeDtypeStruct((nt * TM, D_MODEL), F32),
        scratch_shapes=[pltpu.VMEM((D_MODEL, D_MODEL), BF16)],
        compiler_params=_params("arbitrary"),
        name="mixer_out",
    )(r, w_out, *x_args, gate, ln_g, ln_b)


def _pool_kernel(x_ref, xp_ref, xn_ref, sh_ref, sc_ref, gate_ref, w_ref, ps_ref, g_ref, b_ref, o_ref, ext):
    i = pl.program_id(0)
    has_prev, has_next = _halo_flags(i, NT_ALL)
    one_sc = 1.0 + sc_ref[0]
    sh = sh_ref[0]
    x = x_ref[...]
    h = x * one_sc + sh
    ext[0:HALO] = (xp_ref[...] * one_sc + sh) * has_prev
    ext[HALO:HALO + TM] = h
    ext[HALO + TM:] = (xn_ref[...] * one_sc + sh) * has_next
    t_loc = lax.broadcasted_iota(jnp.int32, (TM, 1), 0) + jnp.maximum(i - 1, 0) * TM
    n_seq = jnp.where(i == 0, CTX_LEN, SEQ)
    mixed = []
    for g, win in enumerate(POOL_WINDOWS):
        c0 = g * POOL_GROUP
        half = win // 2
        s = ext[HALO - half:HALO - half + TM, c0:c0 + POOL_GROUP]
        for j in range(1 - half, win - half):
            s = s + ext[HALO + j:HALO + j + TM, c0:c0 + POOL_GROUP]
        lo = jnp.maximum(t_loc - half, 0)
        hi = jnp.minimum(t_loc + (win - half), n_seq)
        cnt = (hi - lo).astype(F32)
        pooled = s / cnt - h[:, c0:c0 + POOL_GROUP]
        mixed.append(_dot(pooled.astype(BF16), w_ref[g]))
    o = jnp.concatenate(mixed, axis=1) * ps_ref[...]
    y = ALPHA * x + gate_ref[0] * o
    o_ref[...] = _layer_norm(y, g_ref[...], b_ref[...])


def _pool_layer(x, shift, scale, gate, w_grp, pool_scale, ln_g, ln_b):
    prev, nxt = _halo_specs(D_MODEL)
    return pl.pallas_call(
        _pool_kernel,
        grid=(NT_ALL,),
        in_specs=[_row_spec(D_MODEL), prev, nxt, _mod_spec(0), _mod_spec(0), _mod_spec(0),
                  _full_spec((len(POOL_WINDOWS), POOL_GROUP, POOL_GROUP)), _full_spec((1, D_MODEL)),
                  _full_spec((1, D_MODEL)), _full_spec((1, D_MODEL))],
        out_specs=_row_spec(D_MODEL),
        out_shape=jax.ShapeDtypeStruct((T_ALL, D_MODEL), F32),
        scratch_shapes=[pltpu.VMEM((TM + 2 * HALO, D_MODEL), F32)],
        compiler_params=_params("arbitrary"),
        name="pool_layer",
    )(x, x, x, shift, scale, gate, w_grp, pool_scale, ln_g, ln_b)


PROJ_TN = 512


def _proj_kernel(*refs, norm, rope, scale, chunk_width, group_heads):
    refs = list(refs)
    h_ref, w_ref = refs[:2]
    rest = refs[2:]
    gain_ref = rest.pop(0) if norm else None
    if rope:
        c_ref, a_ref, b_ref = rest[:3]
        rest = rest[3:]
    o_ref, w_bf16 = rest
    _cast_once(w_ref, w_bf16, pl.program_id(1) == 0)

    def tile_dot(t):
        return _dot(h_ref[t * TM:(t + 1) * TM, :], w_bf16[...])
    acc_next = tile_dot(0)
    for t in range(PROJ_ROW_TILES):
        acc = acc_next
        if t + 1 < PROJ_ROW_TILES:
            acc_next = tile_dot(t + 1)
        rows = slice(t * TM, (t + 1) * TM)
        if chunk_width:
            for hh in range(PROJ_TN // chunk_width):
                o_ref[hh, t] = acc[:, hh * chunk_width:(hh + 1) * chunk_width].T.astype(o_ref.dtype)
            continue
        for hh in range(PROJ_TN // HEAD_DIM):
            xh = acc[:, hh * HEAD_DIM:(hh + 1) * HEAD_DIM]
            if norm:
                ms = jnp.mean(xh * xh, axis=-1, keepdims=True)
                xh = xh * lax.rsqrt(ms + RMS_EPS) * gain_ref[...]
            if rope:
                xh = (xh * c_ref[rows, :] + pltpu.roll(xh, HEAD_DIM - 32, axis=1) * a_ref[rows, :]
                      + pltpu.roll(xh, 32, axis=1) * b_ref[rows, :])
            if scale != 1.0:
                xt = (xh * scale).T.astype(o_ref.dtype)
                if group_heads:
                    o_ref[0, t, :, hh * TM:(hh + 1) * TM] = xt
                else:
                    o_ref[hh * HEAD_DIM:(hh + 1) * HEAD_DIM, rows] = xt
            else:
                o_ref[rows, hh * HEAD_DIM:(hh + 1) * HEAD_DIM] = xh.astype(o_ref.dtype)


PROJ_ROW_TILES = 3


def _project(h, w, col0, width, *, gain=None, rope_tabs=None, scale=1.0, chunk_width=0, group_heads=False):
    assert NT_ALL % PROJ_ROW_TILES == 0
    rt = PROJ_ROW_TILES
    nb = width // PROJ_TN
    off = col0 // PROJ_TN
    in_specs = [pl.BlockSpec((rt * TM, D_MODEL), lambda j, i: (i, 0)),
                pl.BlockSpec((D_MODEL, PROJ_TN), lambda j, i: (0, j + off))]
    args = [h, w]
    if gain is not None:
        in_specs.append(pl.BlockSpec((1, HEAD_DIM), lambda j, i: (0, 0)))
        args.append(gain)
    if rope_tabs is not None:
        in_specs += [pl.BlockSpec((rt * TM, HEAD_DIM), lambda j, i: (i, 0))] * 3
        args += list(rope_tabs)
    if chunk_width:
        per = PROJ_TN // chunk_width
        out_spec = pl.BlockSpec((per, rt, chunk_width, TM), lambda j, i: (j, i, 0, 0))
        out_shape = jax.ShapeDtypeStruct((width // chunk_width, NT_ALL, chunk_width, TM), BF16)
    elif scale != 1.0 and group_heads:
        per = PROJ_TN // HEAD_DIM
        out_spec = pl.BlockSpec((1, rt, HEAD_DIM, per * TM), lambda j, i: (j, i, 0, 0))
        out_shape = jax.ShapeDtypeStruct((nb, NT_ALL, HEAD_DIM, per * TM), BF16)
    elif scale != 1.0:
        out_spec = pl.BlockSpec((PROJ_TN, rt * TM), lambda j, i: (j, i))
        out_shape = jax.ShapeDtypeStruct((width, T_ALL), BF16)
    else:
        out_spec = pl.BlockSpec((rt * TM, PROJ_TN), lambda j, i: (i, j))
        out_shape = jax.ShapeDtypeStruct((T_ALL, width), BF16)
    kern = functools.partial(_proj_kernel, norm=gain is not None, rope=rope_tabs is not None, scale=scale,
                             chunk_width=chunk_width, group_heads=group_heads)
    return pl.pallas_call(
        kern,
        grid=(nb, NT_ALL // rt),
        in_specs=in_specs,
        out_specs=out_spec,
        out_shape=out_shape,
        scratch_shapes=[pltpu.VMEM((D_MODEL, PROJ_TN), BF16)],
        compiler_params=_params("arbitrary", "arbitrary"),
        name="project",
    )(*args)


def _softmax_update(s, vt_c, m_ref, l_ref, acc_ref):
    m_prev = m_ref[...]
    m_new = jnp.maximum(m_prev, jnp.max(s, axis=0, keepdims=True))
    p = jnp.exp2(s - m_new)
    alpha = jnp.exp2(m_prev - m_new)
    l_ref[...] = alpha * l_ref[...] + jnp.sum(p, axis=0, keepdims=True)
    acc_ref[...] = alpha * acc_ref[...] + _dot(vt_c, p.astype(BF16))
    m_ref[...] = m_new


def _key_tile(k_ref, j):
    return k_ref[j * TM:(j + 1) * TM, :]


def _softmax_pipeline(steps, m_ref, l_ref, acc_ref):
    s_next = steps[0][0]()
    for i, (_, vt_c, stream) in enumerate(steps):
        s_cur = s_next
        if i + 1 < len(steps):
            s_next = steps[i + 1][0]()
        _softmax_update(s_cur, vt_c, m_ref.at[stream], l_ref.at[stream], acc_ref.at[stream])


def _attend(make_steps, context_queries, m_ref, l_ref, acc_ref):
    m_ref[...] = jnp.full(m_ref.shape, -jnp.inf, F32)
    l_ref[...] = jnp.zeros(l_ref.shape, F32)
    acc_ref[...] = jnp.zeros(acc_ref.shape, F32)

    if context_queries is not None:
        @pl.when(context_queries)
        def _():
            _softmax_pipeline(make_steps(0), m_ref, l_ref, acc_ref)

    def all_tiles():
        steps = []
        for j in range(NT_ALL):
            steps += make_steps(j)
        _softmax_pipeline(steps, m_ref, l_ref, acc_ref)

    if context_queries is None:
        all_tiles()
    else:
        pl.when(jnp.logical_not(context_queries))(all_tiles)


def _gqa_kernel(qt_ref, k_ref, vt_ref, o_ref, m_ref, l_ref, acc_ref):
    wide = GQA_STREAM_HEADS * TM

    def make_steps(j):
        k_c = _key_tile(k_ref, j)
        vt_c = vt_ref[0, j]
        return [(functools.partial(lambda s: _dot(k_c, qt_ref[0, 0, :, s * wide:(s + 1) * wide]), s), vt_c, s)
                for s in range(GQA_GROUP // GQA_STREAM_HEADS)]
    _attend(make_steps, pl.program_id(1) == 0, m_ref, l_ref, acc_ref)

    for g in range(GQA_GROUP):
        s, c = divmod(g, GQA_STREAM_HEADS)
        o = acc_ref[s, :, c * TM:(c + 1) * TM] / l_ref[s, :, c * TM:(c + 1) * TM]
        o_ref[:, g * HEAD_DIM:(g + 1) * HEAD_DIM] = o.T.astype(o_ref.dtype)


GQA_STREAM_HEADS = 2


def _gqa_attention(qt, k, vt):
    wide = GQA_GROUP * TM
    streams = GQA_GROUP // GQA_STREAM_HEADS
    sw = GQA_STREAM_HEADS * TM
    return pl.pallas_call(
        _gqa_kernel,
        grid=(GQA_KV_HEADS, NT_ALL),
        in_specs=[pl.BlockSpec((1, 1, HEAD_DIM, wide), lambda kv, i: (kv, i, 0, 0)),
                  pl.BlockSpec((T_ALL, HEAD_DIM), lambda kv, i: (0, kv)),
                  pl.BlockSpec((1, NT_ALL, HEAD_DIM, TM), lambda kv, i: (kv, 0, 0, 0))],
        out_specs=pl.BlockSpec((TM, GQA_GROUP * HEAD_DIM), lambda kv, i: (i, kv)),
        out_shape=jax.ShapeDtypeStruct((T_ALL, D_MODEL), BF16),
        scratch_shapes=[pltpu.VMEM((streams, 1, sw), F32), pltpu.VMEM((streams, 1, sw), F32),
                        pltpu.VMEM((streams, HEAD_DIM, sw), F32)],
        compiler_params=_params("arbitrary", "arbitrary"),
        name="gqa_attention",
    )(qt, k, vt)


def _diff_kernel(qt_ref, k_ref, vt_ref, lam_ref, sub_ref, o_ref, m_ref, l_ref, acc_ref, *, lam_init):
    def make_steps(j):
        k_c = _key_tile(k_ref, j)
        vt_c = vt_ref[0, j]
        return [(functools.partial(lambda s: _dot(k_c[:, s * HEAD_DIM:(s + 1) * HEAD_DIM],
                                                  qt_ref[s * HEAD_DIM:(s + 1) * HEAD_DIM, :]), s), vt_c, s)
                for s in range(2)]
    _attend(make_steps, None, m_ref, l_ref, acc_ref)

    lv = lam_ref[...]
    lam = (jnp.exp(jnp.sum(lv[0:1] * lv[1:2], axis=-1, keepdims=True))
           - jnp.exp(jnp.sum(lv[2:3] * lv[3:4], axis=-1, keepdims=True)) + lam_init)
    o = (acc_ref[0] / l_ref[0] - lam * (acc_ref[1] / l_ref[1])).T
    ms = jnp.mean(o * o, axis=-1, keepdims=True)
    o = o * lax.rsqrt(ms + RMS_EPS) * sub_ref[...] * (1.0 - lam_init)
    o_ref[...] = o.astype(o_ref.dtype)


def _diff_attention(qt, k, vt, lam_vecs, subln, lam_init):
    wv = 2 * HEAD_DIM
    return pl.pallas_call(
        functools.partial(_diff_kernel, lam_init=lam_init),
        grid=(DIFF_HEADS, SEQ // TM),
        in_specs=[pl.BlockSpec((wv, TM), lambda hd, i: (hd, i + 1)),
                  pl.BlockSpec((T_ALL, wv), lambda hd, i: (0, hd)),
                  pl.BlockSpec((1, NT_ALL, wv, TM), lambda hd, i: (hd, 0, 0, 0)),
                  pl.BlockSpec((4, HEAD_DIM), lambda hd, i: (0, 0)), pl.BlockSpec((1, wv), lambda hd, i: (0, 0))],
        out_specs=pl.BlockSpec((TM, wv), lambda hd, i: (i, hd)),
        out_shape=jax.ShapeDtypeStruct((SEQ, D_MODEL), BF16),
        scratch_shapes=[pltpu.VMEM((2, 1, TM), F32), pltpu.VMEM((2, 1, TM), F32), pltpu.VMEM((2, wv, TM), F32)],
        compiler_params=_params("arbitrary", "arbitrary"),
        name="diff_attention",
    )(qt, k, vt, lam_vecs, subln)


def _lane_pick(vals, lane, idx):
    return jnp.sum(jnp.where(lane == idx, vals, 0.0), axis=-1, keepdims=True)


def _router_kernel(x_ref, sh_ref, sc_ref, w_ref, b_ref, meta_ref, cnt_ref, carry):
    i = pl.program_id(0)

    @pl.when(i == 0)
    def _():
        carry[...] = jnp.zeros(carry.shape, F32)

    h = x_ref[...] * (1.0 + sc_ref[0]) + sh_ref[0]
    h_hi = h.astype(BF16)
    h_lo = (h - h_hi.astype(F32)).astype(BF16)
    hw = _dot(h_hi, w_ref[...])
    logits = hw[:, 0:LANES] + hw[:, LANES:2 * LANES] + _dot(h_lo, w_ref[:, 0:LANES]) + b_ref[...]
    lane = lax.broadcasted_iota(jnp.int32, (TM, LANES), 1).astype(F32)
    neg = -jnp.inf
    big = float(LANES)

    grp = jnp.where(lane < N_GROUPS, logits, neg)
    gmax = jnp.max(grp, axis=-1, keepdims=True)
    gstar = jnp.min(jnp.where(grp == gmax, lane, big), axis=-1, keepdims=True)
    p_group = 1.0 / jnp.sum(jnp.exp(grp - gmax), axis=-1, keepdims=True)

    e_lo = N_GROUPS + EXPERTS_PER_GROUP * gstar
    own = jnp.logical_and(lane >= e_lo, lane < e_lo + EXPERTS_PER_GROUP)
    le = jnp.where(own, logits, neg)
    v1 = jnp.max(le, axis=-1, keepdims=True)
    i1 = jnp.min(jnp.where(le == v1, lane, big), axis=-1, keepdims=True)
    le2 = jnp.where(lane == i1, neg, le)
    v2 = jnp.max(le2, axis=-1, keepdims=True)
    i2 = jnp.min(jnp.where(le2 == v2, lane, big), axis=-1, keepdims=True)
    t2 = jnp.exp(v2 - v1)
    w1 = p_group / (1.0 + t2)
    w2 = p_group * t2 / (1.0 + t2)
    e1 = i1 - N_GROUPS
    e2 = i2 - N_GROUPS

    chosen = jnp.logical_or(lane == e1, lane == e2)
    onehot = jnp.where(chosen, 1.0, 0.0)
    r_io = lax.broadcasted_iota(jnp.int32, (TM, TM), 0)
    c_io = lax.broadcasted_iota(jnp.int32, (TM, TM), 1)
    tri = jnp.where(c_io < r_io, 1.0, 0.0).astype(BF16)
    before = carry[...] + _dot(tri, onehot.astype(BF16))
    r1 = _lane_pick(before, lane, e1)
    r2 = _lane_pick(before, lane, e2)
    carry[...] = carry[...] + jnp.sum(onehot, axis=0, keepdims=True)

    meta = jnp.where(lane == 0, e1, 0.0)
    for col, val in ((1, e2), (2, r1), (3, r2), (4, w1), (5, w2)):
        meta = jnp.where(lane == col, val, meta)
    meta_ref[...] = meta
    cnt_ref[...] = jnp.broadcast_to(carry[...], cnt_ref.shape)


def _router(x, shift, scale, w_r, b_r, *, kind0):
    nt = x.shape[0] // TM
    return pl.pallas_call(
        _router_kernel,
        grid=(nt,),
        in_specs=[_row_spec(D_MODEL), _mod_spec(kind0), _mod_spec(kind0),
                  _full_spec((D_MODEL, 2 * LANES)), _full_spec((1, LANES))],
        out_specs=[_row_spec(LANES), _full_spec((SUBLANES, LANES))],
        out_shape=[jax.ShapeDtypeStruct((nt * TM, LANES), F32), jax.ShapeDtypeStruct((SUBLANES, LANES), F32)],
        scratch_shapes=[pltpu.VMEM((1, LANES), F32)],
        compiler_params=_params("arbitrary"),
        name="router",
    )(x, shift, scale, w_r, b_r)


def _row_copy(src, src_row, dst, dst_row, sem):
    return pltpu.make_async_copy(src.at[pl.ds(src_row, 1)], dst.at[pl.ds(dst_row, 1)], sem)


def _dispatch_kernel(pos_ref, ztile_ref, x_ref, sh_ref, sc_ref, out_ref, hbuf, zbuf, sem, zsem, *, n_tok):
    i = pl.program_id(0)

    @pl.when(i == 0)
    def _():
        zbuf[...] = jnp.zeros(zbuf.shape, F32)
        for phase in ("start", "wait"):
            for j in range(ztile_ref.shape[0]):
                tile = ztile_ref[j]

                @pl.when(tile >= 0)
                def _():
                    dst = out_ref.at[pl.ds(pl.multiple_of(tile * TM, TM), TM)]
                    getattr(pltpu.make_async_copy(zbuf, dst, zsem), phase)()

    hbuf[...] = x_ref[...] * (1.0 + sc_ref[0]) + sh_ref[0]

    def start(r, carry):
        for k in range(2):
            _row_copy(hbuf, r, out_ref, pos_ref[k * n_tok + i * TM + r], sem).start(priority=k)
        return carry
    lax.fori_loop(0, TM, start, 0, unroll=ROW_DMA_UNROLL)
    for k in range(2):
        _tile_wait(hbuf, out_ref, sem)


ROW_DMA_UNROLL = 8


def _tile_wait(vmem_tile, hbm, sem):
    pltpu.make_async_copy(hbm.at[pl.ds(0, TM)], vmem_tile, sem).wait()


def _dispatch(pos, zero_tiles, x, shift, scale, n_sorted, *, kind0):
    nt = x.shape[0] // TM
    return pl.pallas_call(
        functools.partial(_dispatch_kernel, n_tok=x.shape[0]),
        grid_spec=pltpu.PrefetchScalarGridSpec(
            num_scalar_prefetch=2,
            grid=(nt,),
            in_specs=[_row_spec(D_MODEL), _mod_spec(kind0), _mod_spec(kind0)],
            out_specs=pl.BlockSpec(memory_space=pl.ANY),
            scratch_shapes=[pltpu.VMEM((TM, D_MODEL), F32), pltpu.VMEM((TM, D_MODEL), F32),
                            pltpu.SemaphoreType.DMA, pltpu.SemaphoreType.DMA],
        ),
        out_shape=jax.ShapeDtypeStruct((n_sorted, D_MODEL), F32),
        compiler_params=_params("arbitrary"),
        name="moe_dispatch",
    )(pos, zero_tiles, x, shift, scale)


PLAN_EXPERT = 0
PLAN_FIRST = 1
PLAN_SLOT = 2
PLAN_NEXT = 3
PLAN_ROWS_VALID = 4
PLAN_SRC_TILE = 5
PLAN_ROWS = 6


def _ffn_kernel(plan_ref, a_ref, wg_hbm, wu_hbm, wd_hbm, o_ref, wg_buf, wu_buf, wd_buf, sems):
    i = pl.program_id(0)
    slot = plan_ref[PLAN_SLOT, i]
    rows = plan_ref[PLAN_ROWS_VALID, i]

    def weight_copies(expert, s):
        return [pltpu.make_async_copy(hbm.at[expert], buf.at[s], sems.at[s, j])
                for j, (hbm, buf) in enumerate(((wg_hbm, wg_buf), (wu_hbm, wu_buf), (wd_hbm, wd_buf)))]

    @pl.when(i == 0)
    def _():
        for c in weight_copies(plan_ref[PLAN_EXPERT, 0], 0):
            c.start()

    @pl.when(plan_ref[PLAN_FIRST, i] == 1)
    def _():
        for c in weight_copies(plan_ref[PLAN_EXPERT, i], slot):
            c.wait()
        nxt = plan_ref[PLAN_NEXT, i]

        @pl.when(nxt >= 0)
        def _():
            for c in weight_copies(nxt, 1 - slot):
                c.start()

    @pl.when(rows > 0)
    def _():
        a = a_ref[...].astype(BF16)
        g = _dot(a, wg_buf[slot].astype(BF16))
        u = _dot(a, wu_buf[slot].astype(BF16))
        act = (g * jax.nn.sigmoid(g) * u).astype(BF16)
        o_ref[...] = _dot(act, wd_buf[slot].astype(BF16))

    @pl.when(rows == 0)
    def _():
        o_ref[...] = jnp.zeros(o_ref.shape, F32)


def _expert_ffn(plan, a_sorted, w_gate, w_up, w_down):
    n_tiles = a_sorted.shape[0] // TM
    return pl.pallas_call(
        _ffn_kernel,
        grid_spec=pltpu.PrefetchScalarGridSpec(
            num_scalar_prefetch=1,
            grid=(n_tiles,),
            in_specs=[pl.BlockSpec((TM, D_MODEL), lambda i, plan: (plan[PLAN_SRC_TILE, i], 0)),
                      pl.BlockSpec(memory_space=pl.ANY), pl.BlockSpec(memory_space=pl.ANY),
                      pl.BlockSpec(memory_space=pl.ANY)],
            out_specs=pl.BlockSpec((TM, D_MODEL), lambda i, plan: (i, 0)),
            scratch_shapes=[pltpu.VMEM((2, D_MODEL, EXPERT_FF), F32), pltpu.VMEM((2, D_MODEL, EXPERT_FF), F32),
                            pltpu.VMEM((2, EXPERT_FF, D_MODEL), F32), pltpu.SemaphoreType.DMA((2, 3))],
        ),
        out_shape=jax.ShapeDtypeStruct(a_sorted.shape, F32),
        compiler_params=_params("arbitrary"),
        name="moe_experts",
    )(plan, a_sorted, w_gate, w_up, w_down)


def _combine_kernel(pos_ref, y_ref, x_ref, meta_ref, gate_ref, g_ref, b_ref, *rest, emit_h, nt):
    if emit_h:
        nsh_ref, nsc_ref, o_ref, h_ref, bufs, sems = rest
    else:
        o_ref, bufs, sems = rest
    i = pl.program_id(0)

    def gather(tile, slot):
        def start(r, carry):
            for k in range(2):
                _row_copy(y_ref, pos_ref[k * (nt * TM) + tile * TM + r], bufs.at[slot, k], r, sems.at[slot]).start()
            return carry
        lax.fori_loop(0, TM, start, 0, unroll=ROW_DMA_UNROLL)

    @pl.when(i == 0)
    def _():
        gather(0, 0)

    @pl.when(i + 1 < nt)
    def _():
        gather(i + 1, (i + 1) & 1)

    slot = i & 1
    for k in range(2):
        _tile_wait(bufs.at[slot, k], y_ref, sems.at[slot])

    meta = meta_ref[...]
    moe = meta[:, 4:5] * bufs[slot, 0] + meta[:, 5:6] * bufs[slot, 1]
    y = ALPHA * x_ref[...] + gate_ref[0] * moe
    out = _layer_norm(y, g_ref[...], b_ref[...])
    o_ref[...] = out
    if emit_h:
        h_ref[...] = (out * (1.0 + nsc_ref[0]) + nsh_ref[0]).astype(h_ref.dtype)


def _combine(pos, y_sorted, x, meta, gate, ln_g, ln_b, next_mod, *, kind0):
    nt = x.shape[0] // TM
    emit_h = next_mod is not None
    mspec = _mod_spec(kind0)
    vec = _full_spec((1, D_MODEL))
    ospec = _row_spec(D_MODEL)
    in_specs = [pl.BlockSpec(memory_space=pl.ANY), ospec, _row_spec(LANES), mspec, vec, vec]
    args = [pos, y_sorted, x, meta, gate, ln_g, ln_b]
    out_specs = [ospec]
    out_shape = [jax.ShapeDtypeStruct((nt * TM, D_MODEL), F32)]
    if emit_h:
        in_specs += [mspec, mspec]
        args += list(next_mod)
        out_specs.append(ospec)
        out_shape.append(jax.ShapeDtypeStruct((nt * TM, D_MODEL), BF16))
    res = pl.pallas_call(
        functools.partial(_combine_kernel, emit_h=emit_h, nt=nt),
        grid_spec=pltpu.PrefetchScalarGridSpec(
            num_scalar_prefetch=1,
            grid=(nt,),
            in_specs=in_specs,
            out_specs=out_specs,
            scratch_shapes=[pltpu.VMEM((2, 2, TM, D_MODEL), F32), pltpu.SemaphoreType.DMA((2,))],
        ),
        out_shape=out_shape,
        compiler_params=_params("arbitrary"),
        name="moe_combine",
    )(*args)
    return res if emit_h else (res[0], None)


def _moe_layer(x, mods, w_r, b_r, w_gate, w_up, w_down, ln_g, ln_b, next_mod, *, kind0, expert0):
    shift, scale, gate = mods
    n_tok = x.shape[0]
    n_tiles = -(-(2 * n_tok + N_EXPERTS * (TM - 1)) // TM)
    meta, cnt = _router(x, shift, scale, w_r, b_r, kind0=kind0)

    i32 = jnp.int32
    counts = cnt[0, :N_EXPERTS].astype(i32)
    padded = ((counts + TM - 1) // TM) * TM
    ends = jnp.cumsum(padded)
    offs = ends - padded
    meta_t = meta[:, :SUBLANES].T
    experts, ranks = meta_t[0:2].astype(i32), meta_t[2:4].astype(i32)
    ids = jnp.arange(N_EXPERTS, dtype=i32)
    own_offs = jnp.sum(jnp.where(experts[..., None] == ids, offs, 0), axis=-1)
    pos = (own_offs + ranks).reshape(-1)

    n_valid = ends[-1] // TM
    tiles = jnp.arange(n_tiles, dtype=i32)
    tile_start = tiles * TM
    tile_valid = tile_start < ends[-1]
    te = jnp.sum(tile_start[:, None] >= ends[None, :], axis=1).astype(i32)
    te_last = jnp.sum((n_valid - 1) * TM >= ends).astype(i32)
    te = jnp.where(tile_valid, te, te_last)
    nonempty = counts > 0
    run = jnp.cumsum(nonempty.astype(i32)) - 1
    later = jnp.where(nonempty[None, :] & (ids[None, :] > ids[:, None]), ids[None, :], N_EXPERTS)
    nxt = jnp.min(later, axis=1)
    nxt = jnp.where(nxt < N_EXPERTS, nxt + expert0, -1)
    plan = jnp.stack([
        te + expert0,
        (tile_valid & (tile_start == offs[te])).astype(i32),
        run[te] & 1,
        nxt[te],
        jnp.where(tile_valid, jnp.clip(counts[te] - (tile_start - offs[te]), 0, TM), 0),
        jnp.minimum(tiles, n_valid - 1),
    ]).astype(i32)
    assert plan.shape[0] == PLAN_ROWS

    partial = jnp.where((counts % TM) != 0, ends // TM - 1, -1)
    tail = n_valid + jnp.arange(n_tiles - 2 * (n_tok // TM), dtype=i32)
    zero_tiles = jnp.concatenate([partial, jnp.where(tail < n_tiles, tail, -1)]).astype(i32)

    a_sorted = _dispatch(pos, zero_tiles, x, shift, scale, n_tiles * TM, kind0=kind0)
    y_sorted = _expert_ffn(plan, a_sorted, w_gate, w_up, w_down)
    return _combine(pos, y_sorted, x, meta, gate, ln_g, ln_b, next_mod, kind0=kind0)


def _rope_tables():
    rows = SEQ // GRID_W
    axis_dim = HEAD_DIM // 2
    inv = ROPE_THETA ** (-jnp.arange(0, axis_dim, 2, dtype=F32) / axis_dim)
    row_ang = jnp.arange(rows, dtype=F32)[:, None] * inv
    col_ang = jnp.arange(GRID_W, dtype=F32)[:, None] * inv
    cr, sr = (jnp.repeat(f(row_ang), GRID_W, axis=0) for f in (jnp.cos, jnp.sin))
    cc, sc = (jnp.tile(f(col_ang), (rows, 1)) for f in (jnp.cos, jnp.sin))
    z = jnp.zeros_like(sr)
    c_tab = jnp.concatenate([cr, cr, cc, cc], axis=1)
    a_tab = jnp.concatenate([-sr, z, -sc, z], axis=1)
    b_tab = jnp.concatenate([z, sr, z, sc], axis=1)
    ctx = (jnp.ones((CTX_LEN, HEAD_DIM), F32), jnp.zeros((CTX_LEN, HEAD_DIM), F32), jnp.zeros((CTX_LEN, HEAD_DIM), F32))
    return tuple(jnp.concatenate([c0, t], axis=0) for c0, t in zip(ctx, (c_tab, a_tab, b_tab)))


def kernel(x, c, ctx, c_ctx, mod_w, mod_b, ln_g, ln_b, rt_grp_w, rt_grp_b, rt_exp_w, rt_exp_b,
           ex_w_gate, ex_w_up, ex_w_down, conv_in_w, conv_w, conv_out_w, pool_w, pool_scale,
           gqa_qkv_w, gqa_qk_norm, gqa_out_w, diff_qkv_w, diff_lambda, diff_subln, diff_out_w):
    assert x.shape == (1, SEQ, D_MODEL) and ctx.shape == (1, CTX_LEN, D_MODEL)
    xs = (ctx[0], x[0])

    c_cols = jnp.pad(jnp.stack([c_ctx, c[0]], axis=1), ((0, 0), (0, LANES - 2)))
    mods = _modulation(c_cols, mod_w, mod_b)
    mods = mods[:, :2].reshape(DEPTH, 2, N_MOD, 1, D_MODEL)
    mod = lambda layer, m: mods[layer, :, m]

    n_pad = LANES - N_GROUPS - N_EXPERTS
    w_r = jnp.concatenate([rt_grp_w, rt_exp_w, jnp.zeros((DEPTH, D_MODEL, n_pad), F32)], axis=-1)
    w_hi = w_r.astype(BF16)
    w_r = jnp.concatenate([w_hi, (w_r - w_hi.astype(F32)).astype(BF16)], axis=-1)
    b_r = jnp.concatenate([rt_grp_b, rt_exp_b, jnp.zeros((DEPTH, n_pad), F32)], axis=-1)[:, None, :]
    rope = _rope_tables()
    sm_scale = HEAD_DIM ** -0.5 * math.log2(math.e)
    bf = lambda w: w.astype(BF16)

    stack = lambda w: w.reshape((DEPTH * N_EXPERTS,) + w.shape[2:])
    w_gate, w_up, w_down = stack(ex_w_gate), stack(ex_w_up), stack(ex_w_down)

    def moe(layer, x_in, next_mod, kind0=0):
        return _moe_layer(x_in, (mod(layer, 3), mod(layer, 4), mod(layer, 5)), w_r[layer], b_r[layer],
                          w_gate, w_up, w_down, ln_g[layer, 1:2], ln_b[layer, 1:2], next_mod,
                          kind0=kind0, expert0=layer * N_EXPERTS)

    h = _modulate(xs, mod(0, 0), mod(0, 1))
    gb, p = _conv_in(h, conv_in_w[0])
    r = _conv_mix(p, gb, conv_w[0])
    xs = _mixer_out(r, conv_out_w[0], xs, mod(0, 2), ln_g[0, 0:1], ln_b[0, 0:1], kind0=0)
    xs, _ = moe(0, xs, None)

    xs = _pool_layer(xs, mod(1, 0), mod(1, 1), mod(1, 2), bf(pool_w[0]), pool_scale[0:1],
                     ln_g[1, 0:1], ln_b[1, 0:1])
    xs, h = moe(1, xs, (mod(2, 0), mod(2, 1)))

    wqkv = gqa_qkv_w[0]
    nq, nkv = GQA_HEADS * HEAD_DIM, GQA_KV_HEADS * HEAD_DIM
    assert PROJ_TN == GQA_GROUP * HEAD_DIM
    qt = _project(h, wqkv, 0, nq, gain=gqa_qk_norm[0, 0:1], rope_tabs=rope, scale=sm_scale, group_heads=True)
    k = _project(h, wqkv, nq, nkv, gain=gqa_qk_norm[0, 1:2], rope_tabs=rope)
    vt = _project(h, wqkv, nq + nkv, nkv, chunk_width=HEAD_DIM)
    o = _gqa_attention(qt, k, vt)
    xs = _mixer_out(o, gqa_out_w[0], xs, mod(2, 2), ln_g[2, 0:1], ln_b[2, 0:1], kind0=0)
    xs, h = moe(2, xs, (mod(3, 0), mod(3, 1)))

    lam_init = 0.8 - 0.6 * math.exp(-0.3 * 3)
    wqkv = diff_qkv_w[0]
    qt = _project(h, wqkv, 0, D_MODEL, rope_tabs=rope, scale=sm_scale)
    k = _project(h, wqkv, D_MODEL, D_MODEL, rope_tabs=rope)
    vt = _project(h, wqkv, 2 * D_MODEL, D_MODEL, chunk_width=2 * HEAD_DIM)
    o = _diff_attention(qt, k, vt, diff_lambda[0], diff_subln[0:1], lam_init)
    xs = _mixer_out(o, diff_out_w[0], xs, mod(3, 2), ln_g[3, 0:1], ln_b[3, 0:1], kind0=1)
    out, _ = moe(3, xs, None, kind0=1)
    return out[None]
```

```python
import functools
import math

import jax
import jax.numpy as jnp
from jax import lax
from jax.experimental import pallas as pl
from jax.experimental.pallas import tpu as pltpu

D_MODEL = 2048
SEQ = 8192
DEPTH = 4
GRID_W = 64
CTX_LEN = 256
ALPHA = (2.0 * DEPTH) ** 0.25
ROPE_THETA = 10000.0
LN_EPS = 1e-6
RMS_EPS = 1e-6
N_MOD = 6
HEAD_DIM = 128
GQA_HEADS = D_MODEL // HEAD_DIM
GQA_KV_HEADS = GQA_HEADS // 4
GQA_GROUP = GQA_HEADS // GQA_KV_HEADS
DIFF_HEADS = D_MODEL // (2 * HEAD_DIM)
POOL_WINDOWS = (2, 4, 8, 16)
POOL_GROUP = D_MODEL // len(POOL_WINDOWS)
N_GROUPS = 4
EXPERTS_PER_GROUP = 4
N_EXPERTS = N_GROUPS * EXPERTS_PER_GROUP
EXPERT_FF = 3 * D_MODEL // 8

LANES = 128
SUBLANES = 8
TM = 256
T_ALL = CTX_LEN + SEQ
NT_ALL = T_ALL // TM
HALO = SUBLANES
VMEM_LIMIT = 56 * 1024 * 1024

assert CTX_LEN == TM and SEQ % TM == 0 and DEPTH == 4

F32 = jnp.float32
BF16 = jnp.bfloat16


def _params(*sem):
    return pltpu.CompilerParams(dimension_semantics=sem, vmem_limit_bytes=VMEM_LIMIT)


def _dot(a, b):
    return jnp.dot(a, b, preferred_element_type=F32)


def _layer_norm(y, g, b):
    mu = jnp.mean(y, axis=-1, keepdims=True)
    d = y - mu
    var = jnp.mean(d * d, axis=-1, keepdims=True)
    return d * lax.rsqrt(var + LN_EPS) * g + b


def _mod_spec(kind0):
    return pl.BlockSpec((1, 1, D_MODEL), lambda i, *_: (jnp.minimum(i + kind0, 1), 0, 0))


def _row_spec(width, tile0=0):
    return pl.BlockSpec((TM, width), lambda i, *_: (i + tile0, 0))


def _full_spec(shape):
    nd = len(shape)
    return pl.BlockSpec(shape, lambda i, *_: (0,) * nd)


MOD_TN = 1024


def _mod_kernel(c_ref, w_ref, b_ref, o_ref):
    w = w_ref[0]
    rows = []
    for r in range(2):
        c = c_ref[:, r:r + 1]
        s = c * jax.nn.sigmoid(c)
        rows.append(jnp.sum(w * s, axis=0, keepdims=True))
    rows.append(jnp.zeros((SUBLANES - 2, MOD_TN), F32))
    o_ref[0] = jnp.concatenate(rows, axis=0) + b_ref[0]


def _modulation(c_cols, mod_w, mod_b):
    n_out = N_MOD * D_MODEL
    return pl.pallas_call(
        _mod_kernel,
        grid=(DEPTH, n_out // MOD_TN),
        in_specs=[
            pl.BlockSpec((D_MODEL, LANES), lambda l, j: (0, 0)),
            pl.BlockSpec((1, D_MODEL, MOD_TN), lambda l, j: (l, 0, j)),
            pl.BlockSpec((1, 1, MOD_TN), lambda l, j: (l, 0, j)),
        ],
        out_specs=pl.BlockSpec((1, SUBLANES, MOD_TN), lambda l, j: (l, 0, j)),
        out_shape=jax.ShapeDtypeStruct((DEPTH, SUBLANES, n_out), F32),
        compiler_params=_params("arbitrary", "arbitrary"),
        name="modulation",
    )(c_cols, mod_w, mod_b.reshape(DEPTH, 1, n_out))


def _stream_specs(x, tile0=0):
    if isinstance(x, tuple):
        return ([pl.BlockSpec((TM, D_MODEL), lambda i, *_: (0, 0)),
                 pl.BlockSpec((TM, D_MODEL), lambda i, *_: (jnp.maximum(i - 1, 0), 0))], list(x))
    return [_row_spec(D_MODEL, tile0)], [x]


def _stream_tile(x_refs):
    if len(x_refs) == 2:
        return jnp.where(pl.program_id(0) == 0, x_refs[0][...], x_refs[1][...])
    return x_refs[0][...]


def _cast_once(w_ref, w_bf16, first_step):
    @pl.when(first_step)
    def _():
        w_bf16[...] = w_ref[...].astype(BF16)


def _modulate_kernel(*refs):
    *x_refs, sh_ref, sc_ref, o_ref = refs
    o_ref[...] = (_stream_tile(x_refs) * (1.0 + sc_ref[0]) + sh_ref[0]).astype(o_ref.dtype)


def _modulate(x, shift, scale):
    x_specs, x_args = _stream_specs(x)
    return pl.pallas_call(
        _modulate_kernel,
        grid=(NT_ALL,),
        in_specs=x_specs + [_mod_spec(0), _mod_spec(0)],
        out_specs=_row_spec(D_MODEL),
        out_shape=jax.ShapeDtypeStruct((T_ALL, D_MODEL), BF16),
        compiler_params=_params("arbitrary"),
        name="modulate",
    )(*x_args, shift, scale)


CONV_TN = 512


def _conv_in_kernel(h_ref, wb_ref, wc_ref, wv_ref, gb_ref, p_ref, w_bf16):
    for s, w_ref in enumerate((wb_ref, wc_ref, wv_ref)):
        _cast_once(w_ref, w_bf16.at[s], pl.program_id(1) == 0)
    h = h_ref[...]
    gb_ref[...] = _dot(h, w_bf16[0])
    p_ref[...] = _dot(h, w_bf16[1]) * _dot(h, w_bf16[2])


CONV_ROW_TILES = 3


def _conv_in(h, w_in):
    assert NT_ALL % CONV_ROW_TILES == 0
    rows = CONV_ROW_TILES * TM
    nb = D_MODEL // CONV_TN
    wspec = lambda off: pl.BlockSpec((D_MODEL, CONV_TN), lambda j, i: (0, j + off * nb))
    ospec = pl.BlockSpec((rows, CONV_TN), lambda j, i: (i, j))
    return pl.pallas_call(
        _conv_in_kernel,
        grid=(nb, T_ALL // rows),
        in_specs=[pl.BlockSpec((rows, D_MODEL), lambda j, i: (i, 0)), wspec(0), wspec(1), wspec(2)],
        out_specs=[ospec, ospec],
        out_shape=[jax.ShapeDtypeStruct((T_ALL, D_MODEL), F32)] * 2,
        scratch_shapes=[pltpu.VMEM((3, D_MODEL, CONV_TN), BF16)],
        compiler_params=_params("arbitrary", "arbitrary"),
        name="conv_in",
    )(h, w_in, w_in, w_in)


def _halo_flags(i, nt):
    has_prev = (i >= 2).astype(F32)
    has_next = jnp.logical_and(i >= 1, i < nt - 1).astype(F32)
    return has_prev, has_next


def _conv_mix_kernel(p_ref, pp_ref, pn_ref, gb_ref, cw_ref, r_ref, ext):
    has_prev, has_next = _halo_flags(pl.program_id(0), NT_ALL)
    ext[0:HALO] = pp_ref[...] * has_prev
    ext[HALO:HALO + TM] = p_ref[...]
    ext[HALO + TM:] = pn_ref[...] * has_next
    q = (ext[HALO - 1:HALO - 1 + TM] * cw_ref[0:1]
         + ext[HALO:HALO + TM] * cw_ref[1:2]
         + ext[HALO + 1:HALO + 1 + TM] * cw_ref[2:3])
    r_ref[...] = (gb_ref[...] * q).astype(r_ref.dtype)


def _halo_specs(width):
    per = TM // HALO
    last = T_ALL // HALO - 1
    prev = pl.BlockSpec((HALO, width), lambda i: (jnp.maximum(i * per - 1, 0), 0))
    nxt = pl.BlockSpec((HALO, width), lambda i: (jnp.minimum((i + 1) * per, last), 0))
    return prev, nxt


def _conv_mix(p, gb, conv_w):
    prev, nxt = _halo_specs(D_MODEL)
    return pl.pallas_call(
        _conv_mix_kernel,
        grid=(NT_ALL,),
        in_specs=[_row_spec(D_MODEL), prev, nxt, _row_spec(D_MODEL), _full_spec((3, D_MODEL))],
        out_specs=_row_spec(D_MODEL),
        out_shape=jax.ShapeDtypeStruct((T_ALL, D_MODEL), BF16),
        scratch_shapes=[pltpu.VMEM((TM + 2 * HALO, D_MODEL), F32)],
        compiler_params=_params("arbitrary"),
        name="conv_mix",
    )(p, p, p, gb, conv_w)


def _mixer_out_kernel(*refs):
    r_ref, w_ref, *x_refs, gate_ref, g_ref, b_ref, o_ref, w_bf16 = refs
    _cast_once(w_ref, w_bf16, pl.program_id(0) == 0)
    o = _dot(r_ref[...], w_bf16[...])
    y = ALPHA * _stream_tile(x_refs) + gate_ref[0] * o
    o_ref[...] = _layer_norm(y, g_ref[...], b_ref[...])


def _mixer_out(r, w_out, x, gate, ln_g, ln_b, *, kind0):
    nt = r.shape[0] // TM
    x_specs, x_args = _stream_specs(x, NT_ALL - nt)
    w_spec = pl.BlockSpec((D_MODEL, D_MODEL), lambda i: (0, 0), pipeline_mode=pl.Buffered(1))
    return pl.pallas_call(
        _mixer_out_kernel,
        grid=(nt,),
        in_specs=[_row_spec(D_MODEL), w_spec] + x_specs
                 + [_mod_spec(kind0), _full_spec((1, D_MODEL)), _full_spec((1, D_MODEL))],
        out_specs=_row_spec(D_MODEL),
        out_shape=jax.ShapeDtypeStruct((nt * TM, D_MODEL), F32),
        scratch_shapes=[pltpu.VMEM((D_MODEL, D_MODEL), BF16)],
        compiler_params=_params("arbitrary"),
        name="mixer_out",
    )(r, w_out, *x_args, gate, ln_g, ln_b)


def _pool_kernel(x_ref, xp_ref, xn_ref, sh_ref, sc_ref, gate_ref, w_ref, ps_ref, g_ref, b_ref, o_ref, ext):
    i = pl.program_id(0)
    has_prev, has_next = _halo_flags(i, NT_ALL)
    one_sc = 1.0 + sc_ref[0]
    sh = sh_ref[0]
    x = x_ref[...]
    h = x * one_sc + sh
    ext[0:HALO] = (xp_ref[...] * one_sc + sh) * has_prev
    ext[HALO:HALO + TM] = h
    ext[HALO + TM:] = (xn_ref[...] * one_sc + sh) * has_next
    t_loc = lax.broadcasted_iota(jnp.int32, (TM, 1), 0) + jnp.maximum(i - 1, 0) * TM
    n_seq = jnp.where(i == 0, CTX_LEN, SEQ)
    mixed = []
    for g, win in enumerate(POOL_WINDOWS):
        c0 = g * POOL_GROUP
        half = win // 2
        s = ext[HALO - half:HALO - half + TM, c0:c0 + POOL_GROUP]
        for j in range(1 - half, win - half):
            s = s + ext[HALO + j:HALO + j + TM, c0:c0 + POOL_GROUP]
        lo = jnp.maximum(t_loc - half, 0)
        hi = jnp.minimum(t_loc + (win - half), n_seq)
        cnt = (hi - lo).astype(F32)
        pooled = s / cnt - h[:, c0:c0 + POOL_GROUP]
        mixed.append(_dot(pooled.astype(BF16), w_ref[g]))
    o = jnp.concatenate(mixed, axis=1) * ps_ref[...]
    y = ALPHA * x + gate_ref[0] * o
    o_ref[...] = _layer_norm(y, g_ref[...], b_ref[...])


def _pool_layer(x, shift, scale, gate, w_grp, pool_scale, ln_g, ln_b):
    prev, nxt = _halo_specs(D_MODEL)
    return pl.pallas_call(
        _pool_kernel,
        grid=(NT_ALL,),
        in_specs=[_row_spec(D_MODEL), prev, nxt, _mod_spec(0), _mod_spec(0), _mod_spec(0),
                  _full_spec((len(POOL_WINDOWS), POOL_GROUP, POOL_GROUP)), _full_spec((1, D_MODEL)),
                  _full_spec((1, D_MODEL)), _full_spec((1, D_MODEL))],
        out_specs=_row_spec(D_MODEL),
        out_shape=jax.ShapeDtypeStruct((T_ALL, D_MODEL), F32),
        scratch_shapes=[pltpu.VMEM((TM + 2 * HALO, D_MODEL), F32)],
        compiler_params=_params("arbitrary"),
        name="pool_layer",
    )(x, x, x, shift, scale, gate, w_grp, pool_scale, ln_g, ln_b)


PROJ_TN = 512


def _proj_kernel(*refs, norm, rope, scale, chunk_width, group_heads):
    refs = list(refs)
    h_ref, w_ref = refs[:2]
    rest = refs[2:]
    gain_ref = rest.pop(0) if norm else None
    if rope:
        c_ref, a_ref, b_ref = rest[:3]
        rest = rest[3:]
    o_ref, w_bf16 = rest
    _cast_once(w_ref, w_bf16, pl.program_id(1) == 0)

    def tile_dot(t):
        return _dot(h_ref[t * TM:(t + 1) * TM, :], w_bf16[...])
    acc_next = tile_dot(0)
    for t in range(PROJ_ROW_TILES):
        acc = acc_next
        if t + 1 < PROJ_ROW_TILES:
            acc_next = tile_dot(t + 1)
        rows = slice(t * TM, (t + 1) * TM)
        if chunk_width:
            for hh in range(PROJ_TN // chunk_width):
                o_ref[hh, t] = acc[:, hh * chunk_width:(hh + 1) * chunk_width].T.astype(o_ref.dtype)
            continue
        for hh in range(PROJ_TN // HEAD_DIM):
            xh = acc[:, hh * HEAD_DIM:(hh + 1) * HEAD_DIM]
            if norm:
                ms = jnp.mean(xh * xh, axis=-1, keepdims=True)
                xh = xh * lax.rsqrt(ms + RMS_EPS) * gain_ref[...]
            if rope:
                xh = (xh * c_ref[rows, :] + pltpu.roll(xh, HEAD_DIM - 32, axis=1) * a_ref[rows, :]
                      + pltpu.roll(xh, 32, axis=1) * b_ref[rows, :])
            if scale != 1.0:
                xt = (xh * scale).T.astype(o_ref.dtype)
                if group_heads:
                    o_ref[0, t, :, hh * TM:(hh + 1) * TM] = xt
                else:
                    o_ref[hh * HEAD_DIM:(hh + 1) * HEAD_DIM, rows] = xt
            else:
                o_ref[rows, hh * HEAD_DIM:(hh + 1) * HEAD_DIM] = xh.astype(o_ref.dtype)


PROJ_ROW_TILES = 3


def _project(h, w, col0, width, *, gain=None, rope_tabs=None, scale=1.0, chunk_width=0, group_heads=False):
    assert NT_ALL % PROJ_ROW_TILES == 0
    rt = PROJ_ROW_TILES
    nb = width // PROJ_TN
    off = col0 // PROJ_TN
    in_specs = [pl.BlockSpec((rt * TM, D_MODEL), lambda j, i: (i, 0)),
                pl.BlockSpec((D_MODEL, PROJ_TN), lambda j, i: (0, j + off))]
    args = [h, w]
    if gain is not None:
        in_specs.append(pl.BlockSpec((1, HEAD_DIM), lambda j, i: (0, 0)))
        args.append(gain)
    if rope_tabs is not None:
        in_specs += [pl.BlockSpec((rt * TM, HEAD_DIM), lambda j, i: (i, 0))] * 3
        args += list(rope_tabs)
    if chunk_width:
        per = PROJ_TN // chunk_width
        out_spec = pl.BlockSpec((per, rt, chunk_width, TM), lambda j, i: (j, i, 0, 0))
        out_shape = jax.ShapeDtypeStruct((width // chunk_width, NT_ALL, chunk_width, TM), BF16)
    elif scale != 1.0 and group_heads:
        per = PROJ_TN // HEAD_DIM
        out_spec = pl.BlockSpec((1, rt, HEAD_DIM, per * TM), lambda j, i: (j, i, 0, 0))
        out_shape = jax.ShapeDtypeStruct((nb, NT_ALL, HEAD_DIM, per * TM), BF16)
    elif scale != 1.0:
        out_spec = pl.BlockSpec((PROJ_TN, rt * TM), lambda j, i: (j, i))
        out_shape = jax.ShapeDtypeStruct((width, T_ALL), BF16)
    else:
        out_spec = pl.BlockSpec((rt * TM, PROJ_TN), lambda j, i: (i, j))
        out_shape = jax.ShapeDtypeStruct((T_ALL, width), BF16)
    kern = functools.partial(_proj_kernel, norm=gain is not None, rope=rope_tabs is not None, scale=scale,
                             chunk_width=chunk_width, group_heads=group_heads)
    return pl.pallas_call(
        kern,
        grid=(nb, NT_ALL // rt),
        in_specs=in_specs,
        out_specs=out_spec,
        out_shape=out_shape,
        scratch_shapes=[pltpu.VMEM((D_MODEL, PROJ_TN), BF16)],
        compiler_params=_params("arbitrary", "arbitrary"),
        name="project",
    )(*args)


def _softmax_update(s, vt_c, m_ref, l_ref, acc_ref):
    m_prev = m_ref[...]
    m_new = jnp.maximum(m_prev, jnp.max(s, axis=0, keepdims=True))
    p = jnp.exp2(s - m_new)
    alpha = jnp.exp2(m_prev - m_new)
    l_ref[...] = alpha * l_ref[...] + jnp.sum(p, axis=0, keepdims=True)
    acc_ref[...] = alpha * acc_ref[...] + _dot(vt_c, p.astype(BF16))
    m_ref[...] = m_new


def _key_tile(k_ref, j):
    return k_ref[j * TM:(j + 1) * TM, :]


def _softmax_pipeline(steps, m_ref, l_ref, acc_ref):
    s_next = steps[0][0]()
    for i, (_, vt_c, stream) in enumerate(steps):
        s_cur = s_next
        if i + 1 < len(steps):
            s_next = steps[i + 1][0]()
        _softmax_update(s_cur, vt_c, m_ref.at[stream], l_ref.at[stream], acc_ref.at[stream])


def _attend(make_steps, context_queries, m_ref, l_ref, acc_ref):
    m_ref[...] = jnp.full(m_ref.shape, -jnp.inf, F32)
    l_ref[...] = jnp.zeros(l_ref.shape, F32)
    acc_ref[...] = jnp.zeros(acc_ref.shape, F32)

    if context_queries is not None:
        @pl.when(context_queries)
        def _():
            _softmax_pipeline(make_steps(0), m_ref, l_ref, acc_ref)

    def all_tiles():
        steps = []
        for j in range(NT_ALL):
            steps += make_steps(j)
        _softmax_pipeline(steps, m_ref, l_ref, acc_ref)

    if context_queries is None:
        all_tiles()
    else:
        pl.when(jnp.logical_not(context_queries))(all_tiles)


def _gqa_kernel(qt_ref, k_ref, vt_ref, o_ref, m_ref, l_ref, acc_ref):
    wide = GQA_STREAM_HEADS * TM

    def make_steps(j):
        k_c = _key_tile(k_ref, j)
        vt_c = vt_ref[0, j]
        return [(functools.partial(lambda s: _dot(k_c, qt_ref[0, 0, :, s * wide:(s + 1) * wide]), s), vt_c, s)
                for s in range(GQA_GROUP // GQA_STREAM_HEADS)]
    _attend(make_steps, pl.program_id(1) == 0, m_ref, l_ref, acc_ref)

    for g in range(GQA_GROUP):
        s, c = divmod(g, GQA_STREAM_HEADS)
        o = acc_ref[s, :, c * TM:(c + 1) * TM] / l_ref[s, :, c * TM:(c + 1) * TM]
        o_ref[:, g * HEAD_DIM:(g + 1) * HEAD_DIM] = o.T.astype(o_ref.dtype)


GQA_STREAM_HEADS = 2


def _gqa_attention(qt, k, vt):
    wide = GQA_GROUP * TM
    streams = GQA_GROUP // GQA_STREAM_HEADS
    sw = GQA_STREAM_HEADS * TM
    return pl.pallas_call(
        _gqa_kernel,
        grid=(GQA_KV_HEADS, NT_ALL),
        in_specs=[pl.BlockSpec((1, 1, HEAD_DIM, wide), lambda kv, i: (kv, i, 0, 0)),
                  pl.BlockSpec((T_ALL, HEAD_DIM), lambda kv, i: (0, kv)),
                  pl.BlockSpec((1, NT_ALL, HEAD_DIM, TM), lambda kv, i: (kv, 0, 0, 0))],
        out_specs=pl.BlockSpec((TM, GQA_GROUP * HEAD_DIM), lambda kv, i: (i, kv)),
        out_shape=jax.ShapeDtypeStruct((T_ALL, D_MODEL), BF16),
        scratch_shapes=[pltpu.VMEM((streams, 1, sw), F32), pltpu.VMEM((streams, 1, sw), F32),
                        pltpu.VMEM((streams, HEAD_DIM, sw), F32)],
        compiler_params=_params("arbitrary", "arbitrary"),
        name="gqa_attention",
    )(qt, k, vt)


def _diff_kernel(qt_ref, k_ref, vt_ref, lam_ref, sub_ref, o_ref, m_ref, l_ref, acc_ref, *, lam_init):
    def make_steps(j):
        k_c = _key_tile(k_ref, j)
        vt_c = vt_ref[0, j]
        return [(functools.partial(lambda s: _dot(k_c[:, s * HEAD_DIM:(s + 1) * HEAD_DIM],
                                                  qt_ref[s * HEAD_DIM:(s + 1) * HEAD_DIM, :]), s), vt_c, s)
                for s in range(2)]
    _attend(make_steps, None, m_ref, l_ref, acc_ref)

    lv = lam_ref[...]
    lam = (jnp.exp(jnp.sum(lv[0:1] * lv[1:2], axis=-1, keepdims=True))
           - jnp.exp(jnp.sum(lv[2:3] * lv[3:4], axis=-1, keepdims=True)) + lam_init)
    o = (acc_ref[0] / l_ref[0] - lam * (acc_ref[1] / l_ref[1])).T
    ms = jnp.mean(o * o, axis=-1, keepdims=True)
    o = o * lax.rsqrt(ms + RMS_EPS) * sub_ref[...] * (1.0 - lam_init)
    o_ref[...] = o.astype(o_ref.dtype)


def _diff_attention(qt, k, vt, lam_vecs, subln, lam_init):
    wv = 2 * HEAD_DIM
    return pl.pallas_call(
        functools.partial(_diff_kernel, lam_init=lam_init),
        grid=(DIFF_HEADS, SEQ // TM),
        in_specs=[pl.BlockSpec((wv, TM), lambda hd, i: (hd, i + 1)),
                  pl.BlockSpec((T_ALL, wv), lambda hd, i: (0, hd)),
                  pl.BlockSpec((1, NT_ALL, wv, TM), lambda hd, i: (hd, 0, 0, 0)),
                  pl.BlockSpec((4, HEAD_DIM), lambda hd, i: (0, 0)), pl.BlockSpec((1, wv), lambda hd, i: (0, 0))],
        out_specs=pl.BlockSpec((TM, wv), lambda hd, i: (i, hd)),
        out_shape=jax.ShapeDtypeStruct((SEQ, D_MODEL), BF16),
        scratch_shapes=[pltpu.VMEM((2, 1, TM), F32), pltpu.VMEM((2, 1, TM), F32), pltpu.VMEM((2, wv, TM), F32)],
        compiler_params=_params("arbitrary", "arbitrary"),
        name="diff_attention",
    )(qt, k, vt, lam_vecs, subln)


def _lane_pick(vals, lane, idx):
    return jnp.sum(jnp.where(lane == idx, vals, 0.0), axis=-1, keepdims=True)


def _router_kernel(x_ref, sh_ref, sc_ref, w_ref, b_ref, meta_ref, cnt_ref, carry):
    i = pl.program_id(0)

    @pl.when(i == 0)
    def _():
        carry[...] = jnp.zeros(carry.shape, F32)

    h = x_ref[...] * (1.0 + sc_ref[0]) + sh_ref[0]
    h_hi = h.astype(BF16)
    h_lo = (h - h_hi.astype(F32)).astype(BF16)
    hw = _dot(h_hi, w_ref[...])
    logits = hw[:, 0:LANES] + hw[:, LANES:2 * LANES] + _dot(h_lo, w_ref[:, 0:LANES]) + b_ref[...]
    lane = lax.broadcasted_iota(jnp.int32, (TM, LANES), 1).astype(F32)
    neg = -jnp.inf
    big = float(LANES)

    grp = jnp.where(lane < N_GROUPS, logits, neg)
    gmax = jnp.max(grp, axis=-1, keepdims=True)
    gstar = jnp.min(jnp.where(grp == gmax, lane, big), axis=-1, keepdims=True)
    p_group = 1.0 / jnp.sum(jnp.exp(grp - gmax), axis=-1, keepdims=True)

    e_lo = N_GROUPS + EXPERTS_PER_GROUP * gstar
    own = jnp.logical_and(lane >= e_lo, lane < e_lo + EXPERTS_PER_GROUP)
    le = jnp.where(own, logits, neg)
    v1 = jnp.max(le, axis=-1, keepdims=True)
    i1 = jnp.min(jnp.where(le == v1, lane, big), axis=-1, keepdims=True)
    le2 = jnp.where(lane == i1, neg, le)
    v2 = jnp.max(le2, axis=-1, keepdims=True)
    i2 = jnp.min(jnp.where(le2 == v2, lane, big), axis=-1, keepdims=True)
    t2 = jnp.exp(v2 - v1)
    w1 = p_group / (1.0 + t2)
    w2 = p_group * t2 / (1.0 + t2)
    e1 = i1 - N_GROUPS
    e2 = i2 - N_GROUPS

    chosen = jnp.logical_or(lane == e1, lane == e2)
    onehot = jnp.where(chosen, 1.0, 0.0)
    r_io = lax.broadcasted_iota(jnp.int32, (TM, TM), 0)
    c_io = lax.broadcasted_iota(jnp.int32, (TM, TM), 1)
    tri = jnp.where(c_io < r_io, 1.0, 0.0).astype(BF16)
    before = carry[...] + _dot(tri, onehot.astype(BF16))
    r1 = _lane_pick(before, lane, e1)
    r2 = _lane_pick(before, lane, e2)
    carry[...] = carry[...] + jnp.sum(onehot, axis=0, keepdims=True)

    meta = jnp.where(lane == 0, e1, 0.0)
    for col, val in ((1, e2), (2, r1), (3, r2), (4, w1), (5, w2)):
        meta = jnp.where(lane == col, val, meta)
    meta_ref[...] = meta
    cnt_ref[...] = jnp.broadcast_to(carry[...], cnt_ref.shape)


def _router(x, shift, scale, w_r, b_r, *, kind0):
    nt = x.shape[0] // TM
    return pl.pallas_call(
        _router_kernel,
        grid=(nt,),
        in_specs=[_row_spec(D_MODEL), _mod_spec(kind0), _mod_spec(kind0),
                  _full_spec((D_MODEL, 2 * LANES)), _full_spec((1, LANES))],
        out_specs=[_row_spec(LANES), _full_spec((SUBLANES, LANES))],
        out_shape=[jax.ShapeDtypeStruct((nt * TM, LANES), F32), jax.ShapeDtypeStruct((SUBLANES, LANES), F32)],
        scratch_shapes=[pltpu.VMEM((1, LANES), F32)],
        compiler_params=_params("arbitrary"),
        name="router",
    )(x, shift, scale, w_r, b_r)


def _row_copy(src, src_row, dst, dst_row, sem):
    return pltpu.make_async_copy(src.at[pl.ds(src_row, 1)], dst.at[pl.ds(dst_row, 1)], sem)


def _dispatch_kernel(pos_ref, ztile_ref, x_ref, sh_ref, sc_ref, out_ref, hbuf, zbuf, sem, zsem, *, n_tok):
    i = pl.program_id(0)

    @pl.when(i == 0)
    def _():
        zbuf[...] = jnp.zeros(zbuf.shape, F32)
        for phase in ("start", "wait"):
            for j in range(ztile_ref.shape[0]):
                tile = ztile_ref[j]

                @pl.when(tile >= 0)
                def _():
                    dst = out_ref.at[pl.ds(pl.multiple_of(tile * TM, TM), TM)]
                    getattr(pltpu.make_async_copy(zbuf, dst, zsem), phase)()

    hbuf[...] = x_ref[...] * (1.0 + sc_ref[0]) + sh_ref[0]

    def start(r, carry):
        for k in range(2):
            _row_copy(hbuf, r, out_ref, pos_ref[k * n_tok + i * TM + r], sem).start(priority=k)
        return carry
    lax.fori_loop(0, TM, start, 0, unroll=ROW_DMA_UNROLL)
    for k in range(2):
        _tile_wait(hbuf, out_ref, sem)


ROW_DMA_UNROLL = 8


def _tile_wait(vmem_tile, hbm, sem):
    pltpu.make_async_copy(hbm.at[pl.ds(0, TM)], vmem_tile, sem).wait()


def _dispatch(pos, zero_tiles, x, shift, scale, n_sorted, *, kind0):
    nt = x.shape[0] // TM
    return pl.pallas_call(
        functools.partial(_dispatch_kernel, n_tok=x.shape[0]),
        grid_spec=pltpu.PrefetchScalarGridSpec(
            num_scalar_prefetch=2,
            grid=(nt,),
            in_specs=[_row_spec(D_MODEL), _mod_spec(kind0), _mod_spec(kind0)],
            out_specs=pl.BlockSpec(memory_space=pl.ANY),
            scratch_shapes=[pltpu.VMEM((TM, D_MODEL), F32), pltpu.VMEM((TM, D_MODEL), F32),
                            pltpu.SemaphoreType.DMA, pltpu.SemaphoreType.DMA],
        ),
        out_shape=jax.ShapeDtypeStruct((n_sorted, D_MODEL), F32),
        compiler_params=_params("arbitrary"),
        name="moe_dispatch",
    )(pos, zero_tiles, x, shift, scale)


PLAN_EXPERT = 0
PLAN_FIRST = 1
PLAN_SLOT = 2
PLAN_NEXT = 3
PLAN_ROWS_VALID = 4
PLAN_SRC_TILE = 5
PLAN_ROWS = 6


def _ffn_kernel(plan_ref, a_ref, wg_hbm, wu_hbm, wd_hbm, o_ref, wg_buf, wu_buf, wd_buf, sems):
    i = pl.program_id(0)
    slot = plan_ref[PLAN_SLOT, i]
    rows = plan_ref[PLAN_ROWS_VALID, i]

    def weight_copies(expert, s):
        return [pltpu.make_async_copy(hbm.at[expert], buf.at[s], sems.at[s, j])
                for j, (hbm, buf) in enumerate(((wg_hbm, wg_buf), (wu_hbm, wu_buf), (wd_hbm, wd_buf)))]

    @pl.when(i == 0)
    def _():
        for c in weight_copies(plan_ref[PLAN_EXPERT, 0], 0):
            c.start()

    @pl.when(plan_ref[PLAN_FIRST, i] == 1)
    def _():
        for c in weight_copies(plan_ref[PLAN_EXPERT, i], slot):
            c.wait()
        nxt = plan_ref[PLAN_NEXT, i]

        @pl.when(nxt >= 0)
        def _():
            for c in weight_copies(nxt, 1 - slot):
                c.start()

    @pl.when(rows > 0)
    def _():
        a = a_ref[...].astype(BF16)
        g = _dot(a, wg_buf[slot].astype(BF16))
        u = _dot(a, wu_buf[slot].astype(BF16))
        act = (g * jax.nn.sigmoid(g) * u).astype(BF16)
        o_ref[...] = _dot(act, wd_buf[slot].astype(BF16))

    @pl.when(rows == 0)
    def _():
        o_ref[...] = jnp.zeros(o_ref.shape, F32)


def _expert_ffn(plan, a_sorted, w_gate, w_up, w_down):
    n_tiles = a_sorted.shape[0] // TM
    return pl.pallas_call(
        _ffn_kernel,
        grid_spec=pltpu.PrefetchScalarGridSpec(
            num_scalar_prefetch=1,
            grid=(n_tiles,),
            in_specs=[pl.BlockSpec((TM, D_MODEL), lambda i, plan: (plan[PLAN_SRC_TILE, i], 0)),
                      pl.BlockSpec(memory_space=pl.ANY), pl.BlockSpec(memory_space=pl.ANY),
                      pl.BlockSpec(memory_space=pl.ANY)],
            out_specs=pl.BlockSpec((TM, D_MODEL), lambda i, plan: (i, 0)),
            scratch_shapes=[pltpu.VMEM((2, D_MODEL, EXPERT_FF), F32), pltpu.VMEM((2, D_MODEL, EXPERT_FF), F32),
                            pltpu.VMEM((2, EXPERT_FF, D_MODEL), F32), pltpu.SemaphoreType.DMA((2, 3))],
        ),
        out_shape=jax.ShapeDtypeStruct(a_sorted.shape, F32),
        compiler_params=_params("arbitrary"),
        name="moe_experts",
    )(plan, a_sorted, w_gate, w_up, w_down)


def _combine_kernel(pos_ref, y_ref, x_ref, meta_ref, gate_ref, g_ref, b_ref, *rest, emit_h, nt):
    if emit_h:
        nsh_ref, nsc_ref, o_ref, h_ref, bufs, sems = rest
    else:
        o_ref, bufs, sems = rest
    i = pl.program_id(0)

    def row_gather(tile, slot, r):
        for k in range(2):
            _row_copy(y_ref, pos_ref[k * (nt * TM) + tile * TM + r], bufs.at[slot, k], r, sems.at[slot]).start()

    @pl.when(i == 0)
    def _():
        def start(r, carry):
            row_gather(0, 0, r)
            return carry
        lax.fori_loop(0, TM, start, 0, unroll=ROW_DMA_UNROLL)

    slot = i & 1
    for k in range(2):
        _tile_wait(bufs.at[slot, k], y_ref, sems.at[slot])

    meta = meta_ref[...]
    moe = meta[:, 4:5] * bufs[slot, 0] + meta[:, 5:6] * bufs[slot, 1]
    y = ALPHA * x_ref[...] + gate_ref[0] * moe
    out = _layer_norm(y, g_ref[...], b_ref[...])
    o_ref[...] = out
    if emit_h:
        h_ref[...] = (out * (1.0 + nsc_ref[0]) + nsh_ref[0]).astype(h_ref.dtype)

    nxt_tile = jnp.minimum(i + 1, nt - 1)
    nxt_slot = 1 - slot
    for r in range(TM):
        row_gather(nxt_tile, nxt_slot, r)

    @pl.when(i == nt - 1)
    def _():
        for k in range(2):
            _tile_wait(bufs.at[nxt_slot, k], y_ref, sems.at[nxt_slot])


def _combine(pos, y_sorted, x, meta, gate, ln_g, ln_b, next_mod, *, kind0):
    nt = x.shape[0] // TM
    emit_h = next_mod is not None
    mspec = _mod_spec(kind0)
    vec = _full_spec((1, D_MODEL))
    ospec = _row_spec(D_MODEL)
    in_specs = [pl.BlockSpec(memory_space=pl.ANY), ospec, _row_spec(LANES), mspec, vec, vec]
    args = [pos, y_sorted, x, meta, gate, ln_g, ln_b]
    out_specs = [ospec]
    out_shape = [jax.ShapeDtypeStruct((nt * TM, D_MODEL), F32)]
    if emit_h:
        in_specs += [mspec, mspec]
        args += list(next_mod)
        out_specs.append(ospec)
        out_shape.append(jax.ShapeDtypeStruct((nt * TM, D_MODEL), BF16))
    res = pl.pallas_call(
        functools.partial(_combine_kernel, emit_h=emit_h, nt=nt),
        grid_spec=pltpu.PrefetchScalarGridSpec(
            num_scalar_prefetch=1,
            grid=(nt,),
            in_specs=in_specs,
            out_specs=out_specs,
            scratch_shapes=[pltpu.VMEM((2, 2, TM, D_MODEL), F32), pltpu.SemaphoreType.DMA((2,))],
        ),
        out_shape=out_shape,
        compiler_params=_params("arbitrary"),
        name="moe_combine",
    )(*args)
    return res if emit_h else (res[0], None)


def _moe_layer(x, mods, w_r, b_r, w_gate, w_up, w_down, ln_g, ln_b, next_mod, *, kind0, expert0):
    shift, scale, gate = mods
    n_tok = x.shape[0]
    n_tiles = -(-(2 * n_tok + N_EXPERTS * (TM - 1)) // TM)
    meta, cnt = _router(x, shift, scale, w_r, b_r, kind0=kind0)

    i32 = jnp.int32
    counts = cnt[0, :N_EXPERTS].astype(i32)
    padded = ((counts + TM - 1) // TM) * TM
    ends = jnp.cumsum(padded)
    offs = ends - padded
    meta_t = meta[:, :SUBLANES].T
    experts, ranks = meta_t[0:2].astype(i32), meta_t[2:4].astype(i32)
    ids = jnp.arange(N_EXPERTS, dtype=i32)
    own_offs = jnp.sum(jnp.where(experts[..., None] == ids, offs, 0), axis=-1)
    pos = (own_offs + ranks).reshape(-1)

    n_valid = ends[-1] // TM
    tiles = jnp.arange(n_tiles, dtype=i32)
    tile_start = tiles * TM
    tile_valid = tile_start < ends[-1]
    te = jnp.sum(tile_start[:, None] >= ends[None, :], axis=1).astype(i32)
    te_last = jnp.sum((n_valid - 1) * TM >= ends).astype(i32)
    te = jnp.where(tile_valid, te, te_last)
    nonempty = counts > 0
    run = jnp.cumsum(nonempty.astype(i32)) - 1
    later = jnp.where(nonempty[None, :] & (ids[None, :] > ids[:, None]), ids[None, :], N_EXPERTS)
    nxt = jnp.min(later, axis=1)
    nxt = jnp.where(nxt < N_EXPERTS, nxt + expert0, -1)
    plan = jnp.stack([
        te + expert0,
        (tile_valid & (tile_start == offs[te])).astype(i32),
        run[te] & 1,
        nxt[te],
        jnp.where(tile_valid, jnp.clip(counts[te] - (tile_start - offs[te]), 0, TM), 0),
        jnp.minimum(tiles, n_valid - 1),
    ]).astype(i32)
    assert plan.shape[0] == PLAN_ROWS

    partial = jnp.where((counts % TM) != 0, ends // TM - 1, -1)
    tail = n_valid + jnp.arange(n_tiles - 2 * (n_tok // TM), dtype=i32)
    zero_tiles = jnp.concatenate([partial, jnp.where(tail < n_tiles, tail, -1)]).astype(i32)

    a_sorted = _dispatch(pos, zero_tiles, x, shift, scale, n_tiles * TM, kind0=kind0)
    y_sorted = _expert_ffn(plan, a_sorted, w_gate, w_up, w_down)
    return _combine(pos, y_sorted, x, meta, gate, ln_g, ln_b, next_mod, kind0=kind0)


def _rope_tables():
    rows = SEQ // GRID_W
    axis_dim = HEAD_DIM // 2
    inv = ROPE_THETA ** (-jnp.arange(0, axis_dim, 2, dtype=F32) / axis_dim)
    row_ang = jnp.arange(rows, dtype=F32)[:, None] * inv
    col_ang = jnp.arange(GRID_W, dtype=F32)[:, None] * inv
    cr, sr = (jnp.repeat(f(row_ang), GRID_W, axis=0) for f in (jnp.cos, jnp.sin))
    cc, sc = (jnp.tile(f(col_ang), (rows, 1)) for f in (jnp.cos, jnp.sin))
    z = jnp.zeros_like(sr)
    c_tab = jnp.concatenate([cr, cr, cc, cc], axis=1)
    a_tab = jnp.concatenate([-sr, z, -sc, z], axis=1)
    b_tab = jnp.concatenate([z, sr, z, sc], axis=1)
    ctx = (jnp.ones((CTX_LEN, HEAD_DIM), F32), jnp.zeros((CTX_LEN, HEAD_DIM), F32), jnp.zeros((CTX_LEN, HEAD_DIM), F32))
    return tuple(jnp.concatenate([c0, t], axis=0) for c0, t in zip(ctx, (c_tab, a_tab, b_tab)))


def kernel(x, c, ctx, c_ctx, mod_w, mod_b, ln_g, ln_b, rt_grp_w, rt_grp_b, rt_exp_w, rt_exp_b,
           ex_w_gate, ex_w_up, ex_w_down, conv_in_w, conv_w, conv_out_w, pool_w, pool_scale,
           gqa_qkv_w, gqa_qk_norm, gqa_out_w, diff_qkv_w, diff_lambda, diff_subln, diff_out_w):
    assert x.shape == (1, SEQ, D_MODEL) and ctx.shape == (1, CTX_LEN, D_MODEL)
    xs = (ctx[0], x[0])

    c_cols = jnp.pad(jnp.stack([c_ctx, c[0]], axis=1), ((0, 0), (0, LANES - 2)))
    mods = _modulation(c_cols, mod_w, mod_b)
    mods = mods[:, :2].reshape(DEPTH, 2, N_MOD, 1, D_MODEL)
    mod = lambda layer, m: mods[layer, :, m]

    n_pad = LANES - N_GROUPS - N_EXPERTS
    w_r = jnp.concatenate([rt_grp_w, rt_exp_w, jnp.zeros((DEPTH, D_MODEL, n_pad), F32)], axis=-1)
    w_hi = w_r.astype(BF16)
    w_r = jnp.concatenate([w_hi, (w_r - w_hi.astype(F32)).astype(BF16)], axis=-1)
    b_r = jnp.concatenate([rt_grp_b, rt_exp_b, jnp.zeros((DEPTH, n_pad), F32)], axis=-1)[:, None, :]
    rope = _rope_tables()
    sm_scale = HEAD_DIM ** -0.5 * math.log2(math.e)
    bf = lambda w: w.astype(BF16)

    stack = lambda w: w.reshape((DEPTH * N_EXPERTS,) + w.shape[2:])
    w_gate, w_up, w_down = stack(ex_w_gate), stack(ex_w_up), stack(ex_w_down)

    def moe(layer, x_in, next_mod, kind0=0):
        return _moe_layer(x_in, (mod(layer, 3), mod(layer, 4), mod(layer, 5)), w_r[layer], b_r[layer],
                          w_gate, w_up, w_down, ln_g[layer, 1:2], ln_b[layer, 1:2], next_mod,
                          kind0=kind0, expert0=layer * N_EXPERTS)

    h = _modulate(xs, mod(0, 0), mod(0, 1))
    gb, p = _conv_in(h, conv_in_w[0])
    r = _conv_mix(p, gb, conv_w[0])
    xs = _mixer_out(r, conv_out_w[0], xs, mod(0, 2), ln_g[0, 0:1], ln_b[0, 0:1], kind0=0)
    xs, _ = moe(0, xs, None)

    xs = _pool_layer(xs, mod(1, 0), mod(1, 1), mod(1, 2), bf(pool_w[0]), pool_scale[0:1],
                     ln_g[1, 0:1], ln_b[1, 0:1])
    xs, h = moe(1, xs, (mod(2, 0), mod(2, 1)))

    wqkv = gqa_qkv_w[0]
    nq, nkv = GQA_HEADS * HEAD_DIM, GQA_KV_HEADS * HEAD_DIM
    assert PROJ_TN == GQA_GROUP * HEAD_DIM
    qt = _project(h, wqkv, 0, nq, gain=gqa_qk_norm[0, 0:1], rope_tabs=rope, scale=sm_scale, group_heads=True)
    k = _project(h, wqkv, nq, nkv, gain=gqa_qk_norm[0, 1:2], rope_tabs=rope)
    vt = _project(h, wqkv, nq + nkv, nkv, chunk_width=HEAD_DIM)
    o = _gqa_attention(qt, k, vt)
    xs = _mixer_out(o, gqa_out_w[0], xs, mod(2, 2), ln_g[2, 0:1], ln_b[2, 0:1], kind0=0)
    xs, h = moe(2, xs, (mod(3, 0), mod(3, 1)))

    lam_init = 0.8 - 0.6 * math.exp(-0.3 * 3)
    wqkv = diff_qkv_w[0]
    qt = _project(h, wqkv, 0, D_MODEL, rope_tabs=rope, scale=sm_scale)
    k = _project(h, wqkv, D_MODEL, D_MODEL, rope_tabs=rope)
    vt = _project(h, wqkv, 2 * D_MODEL, D_MODEL, chunk_width=2 * HEAD_DIM)
    o = _diff_attention(qt, k, vt, diff_lambda[0], diff_subln[0:1], lam_init)
    xs = _mixer_out(o, diff_out_w[0], xs, mod(3, 2), ln_g[3, 0:1], ln_b[3, 0:1], kind0=1)
    out, _ = moe(3, xs, None, kind0=1)
    return out[None]
```
